```python
import jax, jax.numpy as jnp
from jax import lax
import numpy as np

D_MODEL = 1024
BATCH = 8
SEQ = 4096
DEPTH = 4

N_EVEN = (DEPTH + 1) // 2
N_ODD = DEPTH // 2

A_WIDTH = D_MODEL // 2
A_HEADS = 4
A_HEAD_DIM = A_WIDTH // A_HEADS
MLSTM_CHUNK = 128
CONV_WIDTH = 5

B_WIDTH = D_MODEL - A_WIDTH
B_GROUPS = 4
B_GROUP_DIM = B_WIDTH // B_GROUPS
SGU_CHUNK = 128

AB_IN = 4 * A_WIDTH + 4 * A_HEADS + 2 * B_WIDTH

C_HEADS = 16
C_NOPE = 64
C_ROPE = 32
C_V = 64
Q_LORA = 384
KV_LORA = 256
C_IN = Q_LORA + KV_LORA + C_ROPE
ROPE_BASE = 10000.0
Q_BLOCK = 128

D_FF = 4 * D_MODEL

EPS = 1e-6

kernel_name = "hybrid_mlstm_sgu_mla_encoder"


def rms_norm(x, g):
    xf = x.astype(jnp.float32)
    y = xf * lax.rsqrt(jnp.mean(xf * xf, axis=-1, keepdims=True) + EPS)
    return (y * g.astype(jnp.float32)).astype(x.dtype)


def centred_dwconv(x, w):
    pad = (w.shape[0] - 1) // 2
    return lax.conv_general_dilated(
        x, w[:, None, :].astype(x.dtype), window_strides=(1,),
        padding=[(pad, pad)], dimension_numbers=("NWC", "WIO", "NWC"),
        feature_group_count=x.shape[-1])


def mlstm_scan(q, k, v, log_i, log_f):
    bsz, nh, seq, dh = q.shape
    L = MLSTM_CHUNK
    nc = seq // L

    def to_chunks(t):
        return jnp.moveaxis(t.reshape(bsz, nh, nc, L, *t.shape[3:]), 2, 0)

    lower = jnp.tril(jnp.ones((L, L), dtype=bool))

    def step(carry, xs):
        C, n, m = carry
        qc, kc, vc, li, lf = xs
        b = jnp.cumsum(lf, axis=-1)
        dmat = jnp.where(lower, b[..., :, None] - b[..., None, :] + li[..., None, :], -jnp.inf)
        m_t = jnp.maximum(b + m[..., None], jnp.max(dmat, axis=-1))
        inter = jnp.exp(b + m[..., None] - m_t)
        s = jnp.einsum("bhtd,bhsd->bhts", qc, kc) * jnp.exp(dmat - m_t[..., None])
        num = (jnp.einsum("bhts,bhse->bhte", s, vc)
               + inter[..., None] * jnp.einsum("bhtd,bhde->bhte", qc, C))
        den = jnp.sum(s, axis=-1) + inter * jnp.einsum("bhtd,bhd->bht", qc, n)
        h = num / jnp.maximum(jnp.abs(den), jnp.exp(-m_t))[..., None]
        g = b[..., -1]
        w_log = g[..., None] - b + li
        m_new = jnp.maximum(g + m, jnp.max(w_log, axis=-1))
        decay = jnp.exp(g + m - m_new)
        w = jnp.exp(w_log - m_new[..., None])
        C_new = decay[..., None, None] * C + jnp.einsum("bhs,bhsd,bhse->bhde", w, kc, vc)
        n_new = decay[..., None] * n + jnp.einsum("bhs,bhsd->bhd", w, kc)
        return (C_new, n_new, m_new), h

    init = (jnp.zeros((bsz, nh, dh, dh), jnp.float32),
            jnp.zeros((bsz, nh, dh), jnp.float32),
            jnp.zeros((bsz, nh), jnp.float32))
    _, h = lax.scan(step, init, (to_chunks(q), to_chunks(k), to_chunks(v),
                                 to_chunks(log_i), to_chunks(log_f)))
    return jnp.moveaxis(h, 0, 2).reshape(bsz, nh, seq, dh)


def ab_mixer(h, w_in, conv_w, gate_b, head_g, v_g, ws, bs, w_out):
    bsz, seq, _ = h.shape
    p = h @ w_in
    qk, va, oa, gates, uv = jnp.split(
        p, [2 * A_WIDTH, 3 * A_WIDTH, 4 * A_WIDTH, 4 * A_WIDTH + 4 * A_HEADS], axis=-1)

    qk = jax.nn.silu(centred_dwconv(qk, conv_w))

    def heads(t):
        return t.reshape(bsz, seq, A_HEADS, A_HEAD_DIM).transpose(0, 2, 1, 3).astype(jnp.float32)

    q = heads(qk[..., :A_WIDTH]) * (A_HEAD_DIM ** -0.5)
    k = heads(qk[..., A_WIDTH:])
    v = heads(va)
    g = (gates + gate_b).astype(jnp.float32).reshape(bsz, seq, 4, A_HEADS).transpose(2, 0, 3, 1)
    li_f, lf_f = g[0], jax.nn.log_sigmoid(g[1])
    li_b, lf_b = g[2], jax.nn.log_sigmoid(g[3])
    h_fwd = mlstm_scan(q, k, v, li_f, lf_f)
    fl = lambda t: jnp.flip(t, axis=2)
    h_bwd = fl(mlstm_scan(fl(q), fl(k), fl(v), jnp.flip(li_b, -1), jnp.flip(lf_b, -1)))
    ha = rms_norm(h_fwd + h_bwd, head_g.reshape(A_HEADS, 1, A_HEAD_DIM))
    ha = ha.transpose(0, 2, 1, 3).reshape(bsz, seq, A_WIDTH).astype(h.dtype) * jax.nn.sigmoid(oa)

    u, vb = jnp.split(jax.nn.gelu(uv), 2, axis=-1)
    vb = rms_norm(vb.reshape(bsz, seq, B_GROUPS, B_GROUP_DIM), v_g.reshape(B_GROUPS, B_GROUP_DIM))
    vb = vb.reshape(bsz, seq // SGU_CHUNK, SGU_CHUNK, B_GROUPS, B_GROUP_DIM)
    sp = jnp.einsum("gts,bnsgc->bntgc", ws, vb) + bs.T[:, :, None]
    hb = u * sp.reshape(bsz, seq, B_WIDTH)

    return jnp.concatenate([ha, hb], axis=-1) @ w_out


def apply_rope(x, cos, sin):
    half = x.shape[-1] // 2
    x1, x2 = x[..., :half], x[..., half:]
    return jnp.concatenate([x1 * cos - x2 * sin, x1 * sin + x2 * cos], axis=-1).astype(x.dtype)


def mla_mixer(h, positions, w_in, q_g, kv_g, w_uq, w_ukv, w_out):
    bsz, seq, _ = h.shape
    cq, ckv, kr = jnp.split(h @ w_in, [Q_LORA, Q_LORA + KV_LORA], axis=-1)
    q = (rms_norm(cq, q_g) @ w_uq).reshape(bsz, seq, C_HEADS, C_NOPE + C_ROPE)
    kv = (rms_norm(ckv, kv_g) @ w_ukv).reshape(bsz, seq, C_HEADS, C_NOPE + C_V)

    half = C_ROPE // 2
    freq = ROPE_BASE ** (-jnp.arange(half, dtype=jnp.float32) / half)
    ang = positions.astype(jnp.float32)[..., None] * freq
    cos = jnp.cos(ang)[:, :, None, :]
    sin = jnp.sin(ang)[:, :, None, :]
    q_rope = apply_rope(q[..., C_NOPE:], cos, sin)
    k_rope = apply_rope(kr[:, :, None, :], cos, sin)
    qf = jnp.concatenate([q[..., :C_NOPE], q_rope], axis=-1)
    kf = jnp.concatenate([kv[..., :C_NOPE],
                          jnp.broadcast_to(k_rope, (bsz, seq, C_HEADS, C_ROPE))], axis=-1)
    v = kv[..., C_NOPE:]
    scale = (C_NOPE + C_ROPE) ** -0.5

    qb = jnp.moveaxis(qf.reshape(bsz, seq // Q_BLOCK, Q_BLOCK, C_HEADS, C_NOPE + C_ROPE), 1, 0)

    def attend(qblk):
        s = jnp.einsum("bqhd,bkhd->bhqk", qblk, kf, preferred_element_type=jnp.float32) * scale
        pr = jax.nn.softmax(s, axis=-1)
        return jnp.einsum("bhqk,bkhd->bqhd", pr.astype(v.dtype), v)

    o = lax.map(attend, qb)
    o = jnp.moveaxis(o, 0, 1).reshape(bsz, seq, C_HEADS * C_V)
    return o @ w_out


def setup_inputs(seed: int = 0) -> dict:
    key = jax.random.key(seed)
    ks = iter(jax.random.split(key, 40))
    f32 = jnp.float32

    def nrm(shape, scale):
        return jax.random.normal(next(ks), shape, f32) * scale

    def gain(shape):
        return 1.0 + 0.02 * jax.random.normal(next(ks), shape, f32)

    x = jax.random.normal(next(ks), (BATCH, SEQ, D_MODEL), f32)
    offsets = jax.random.randint(next(ks), (BATCH, 1), 0, 1024, dtype=jnp.int32)
    positions = offsets + jnp.arange(SEQ, dtype=jnp.int32)[None, :]

    forget_b = jnp.linspace(3.0, 6.0, A_HEADS, dtype=f32)
    gate_b = jnp.concatenate([
        nrm((N_EVEN, A_HEADS), 0.1),
        forget_b + nrm((N_EVEN, A_HEADS), 0.1),
        nrm((N_EVEN, A_HEADS), 0.1),
        forget_b + nrm((N_EVEN, A_HEADS), 0.1)], axis=-1)

    return {
        "x": x,
        "positions": positions,
        "ab_norm": gain((N_EVEN, D_MODEL)),
        "ab_w_in": nrm((N_EVEN, D_MODEL, AB_IN), D_MODEL ** -0.5),
        "ab_conv": nrm((N_EVEN, CONV_WIDTH, 2 * A_WIDTH), CONV_WIDTH ** -0.5),
        "ab_gate_b": gate_b,
        "ab_head_g": gain((N_EVEN, A_WIDTH)),
        "ab_v_g": gain((N_EVEN, B_WIDTH)),
        "ab_ws": nrm((N_EVEN, B_GROUPS, SGU_CHUNK, SGU_CHUNK), SGU_CHUNK ** -0.5),
        "ab_bs": gain((N_EVEN, B_GROUPS, SGU_CHUNK)),
        "ab_w_out": nrm((N_EVEN, A_WIDTH + B_WIDTH, D_MODEL), (A_WIDTH + B_WIDTH) ** -0.5),
        "c_norm": gain((N_ODD, D_MODEL)),
        "c_w_in": nrm((N_ODD, D_MODEL, C_IN), D_MODEL ** -0.5),
        "c_q_g": gain((N_ODD, Q_LORA)),
        "c_kv_g": gain((N_ODD, KV_LORA)),
        "c_w_uq": nrm((N_ODD, Q_LORA, C_HEADS * (C_NOPE + C_ROPE)), Q_LORA ** -0.5),
        "c_w_ukv": nrm((N_ODD, KV_LORA, C_HEADS * (C_NOPE + C_V)), KV_LORA ** -0.5),
        "c_w_out": nrm((N_ODD, C_HEADS * C_V, D_MODEL), (C_HEADS * C_V) ** -0.5),
        "ffn_norm": gain((DEPTH, D_MODEL)),
        "ffn_w1": nrm((DEPTH, D_MODEL, D_FF), D_MODEL ** -0.5),
        "ffn_w2": nrm((DEPTH, D_FF, D_MODEL), D_FF ** -0.5),
        "final_norm": gain((D_MODEL,)),
    }


def reference(x, positions, ab_norm, ab_w_in, ab_conv, ab_gate_b, ab_head_g, ab_v_g,
              ab_ws, ab_bs, ab_w_out, c_norm, c_w_in, c_q_g, c_kv_g, c_w_uq, c_w_ukv,
              c_w_out, ffn_norm, ffn_w1, ffn_w2, final_norm):
    for layer in range(DEPTH):
        j = layer // 2
        if layer % 2 == 0:
            x = x + ab_mixer(rms_norm(x, ab_norm[j]), ab_w_in[j], ab_conv[j], ab_gate_b[j],
                             ab_head_g[j], ab_v_g[j], ab_ws[j], ab_bs[j], ab_w_out[j])
        else:
            x = x + mla_mixer(rms_norm(x, c_norm[j]), positions, c_w_in[j], c_q_g[j],
                              c_kv_g[j], c_w_uq[j], c_w_ukv[j], c_w_out[j])
        hf = rms_norm(x, ffn_norm[layer])
        x = x + jnp.square(jax.nn.relu(hf @ ffn_w1[layer])) @ ffn_w2[layer]
    return rms_norm(x, final_norm)
```

```python
import functools

import jax
import jax.numpy as jnp
from jax import lax
from jax.experimental import pallas as pl
from jax.experimental.pallas import tpu as pltpu

F32 = jnp.float32
BF16 = jnp.bfloat16

D_MODEL = 1024
DEPTH = 4
A_WIDTH = 512
A_HEADS = 4
A_HEAD_DIM = 128
CHUNK = 128
CONV_WIDTH = 5
B_WIDTH = 512
B_GROUPS = 4
B_GROUP_DIM = 128
C_HEADS = 16
C_NOPE = 64
C_ROPE = 32
C_V = 64
Q_LORA = 384
KV_LORA = 256
ROPE_BASE = 10000.0
D_FF = 4 * D_MODEL
EPS = 1e-6

LANES = 128
HEAD_PAD = 128
VMEM_LIMIT = 56 * 1024 * 1024


def _params(n_axes):
    return pltpu.CompilerParams(
        dimension_semantics=("arbitrary",) * n_axes, vmem_limit_bytes=VMEM_LIMIT)


def _const_spec(shape):
    nd = len(shape)
    return pl.BlockSpec(shape, lambda *_: (0,) * nd, pipeline_mode=pl.Buffered(1))


def _rms(x, g):
    return x * lax.rsqrt(jnp.mean(x * x, axis=-1, keepdims=True) + EPS) * g


def _dot(a, b):
    return jnp.dot(a, b, preferred_element_type=F32)


def _sigmoid(x):
    return 1.0 / (1.0 + jnp.exp(-x))


def _ab_in_kernel(x_ref, g_ref, wqk_ref, wv_ref, wo_ref, wg_ref, wuv_ref, vg_ref, ws_ref, bs_ref,
                  qk_ref, va_ref, oa_ref, gt_ref, hb_ref):
    tm = x_ref.shape[0]
    h = _rms(x_ref[...], g_ref[...]).astype(BF16)
    qk_ref[...] = _dot(h, wqk_ref[...])
    va_ref[...] = _dot(h, wv_ref[...]).astype(BF16)
    oa_ref[...] = _dot(h, wo_ref[...])
    gt_ref[...] = _dot(h, wg_ref[...])

    uv = _dot(h, wuv_ref[...])
    uv = 0.5 * uv * (1.0 + jnp.tanh(0.7978845608028654 * (uv + 0.044715 * (uv * uv * uv))))
    n_chunks = tm // CHUNK
    for g in range(B_GROUPS):
        lo = g * B_GROUP_DIM
        u = uv[:, lo:lo + B_GROUP_DIM]
        vb = uv[:, B_WIDTH + lo:B_WIDTH + lo + B_GROUP_DIM]
        vb = _rms(vb, vg_ref[:, lo:lo + B_GROUP_DIM]).astype(BF16)
        rhs = jnp.concatenate([vb[c * CHUNK:(c + 1) * CHUNK, :] for c in range(n_chunks)], axis=1)
        sp = _dot(ws_ref[g], rhs) + bs_ref[:, g:g + 1]
        for c in range(n_chunks):
            hb_ref[c * CHUNK:(c + 1) * CHUNK, lo:lo + B_GROUP_DIM] = (
                u[c * CHUNK:(c + 1) * CHUNK, :] * sp[:, c * B_GROUP_DIM:(c + 1) * B_GROUP_DIM]
            ).astype(BF16)


def _ab_in(x2, g, wqk, wv, wo, wg, wuv, vg, ws, bs_t, tm=512):
    n = x2.shape[0]
    row = lambda w: pl.BlockSpec((tm, w), lambda i: (i, 0))
    return pl.pallas_call(
        _ab_in_kernel,
        grid=(n // tm,),
        in_specs=[row(D_MODEL), _const_spec(g.shape), _const_spec(wqk.shape), _const_spec(wv.shape),
                  _const_spec(wo.shape), _const_spec(wg.shape), _const_spec(wuv.shape),
                  _const_spec(vg.shape), _const_spec(ws.shape), _const_spec(bs_t.shape)],
        out_specs=[row(2 * A_WIDTH), row(A_WIDTH), row(A_WIDTH), row(4 * A_HEADS), row(B_WIDTH)],
        out_shape=[jax.ShapeDtypeStruct((n, 2 * A_WIDTH), F32),
                   jax.ShapeDtypeStruct((n, A_WIDTH), BF16),
                   jax.ShapeDtypeStruct((n, A_WIDTH), F32),
                   jax.ShapeDtypeStruct((n, 4 * A_HEADS), F32),
                   jax.ShapeDtypeStruct((n, B_WIDTH), BF16)],
        compiler_params=_params(1),
        name="ab_in",
    )(x2, g, wqk, wv, wo, wg, wuv, vg, ws, bs_t)


def _split3_dot(x, tri):
    x0 = x.astype(BF16)
    r1 = x - x0.astype(F32)
    x1 = r1.astype(BF16)
    x2 = (r1 - x1.astype(F32)).astype(BF16)
    return _dot(x0, tri) + _dot(x1, tri) + _dot(x2, tri)


def _mlstm_kernel(q_ref, k_ref, v_ref, oa_ref, gr_ref, gb_ref, cwq_ref, cwk_ref, hg_ref, o_ref,
                  xp_ref, qs_ref, kt_ref, va_ref, act_ref, pre_ref, suf_ref, hf_ref, hb_ref,
                  cf_ref, cb_ref):
    seq = q_ref.shape[1]
    nc = seq // CHUNK
    dh = A_HEAD_DIM
    halo = 8
    pad = (CONV_WIDTH - 1) // 2

    zeros_halo = jnp.zeros((halo, dh), F32)
    xp_ref[0:halo, :] = zeros_halo
    xp_ref[halo + seq:halo + seq + halo, :] = zeros_halo

    def conv_chunk(c, w):
        r0 = pl.multiple_of(c * CHUNK, CHUNK)
        win = xp_ref[pl.ds(r0, CHUNK + 2 * halo), :]
        y = win[halo - pad:halo - pad + CHUNK, :] * w[0:1, :]
        for j in range(1, CONV_WIDTH):
            y = y + win[halo - pad + j:halo - pad + j + CHUNK, :] * w[j:j + 1, :]
        return y * _sigmoid(y)

    xp_ref[halo:halo + seq, :] = q_ref[0]
    wq = cwq_ref[...]

    def q_body(c, carry):
        r0 = pl.multiple_of(c * CHUNK, CHUNK)
        qs_ref[pl.ds(r0, CHUNK), :] = (conv_chunk(c, wq) * (dh ** -0.5)).astype(BF16)
        return carry

    lax.fori_loop(0, nc, q_body, 0)

    xp_ref[halo:halo + seq, :] = k_ref[0]
    wk = cwk_ref[...]

    def k_body(c, carry):
        r0 = pl.multiple_of(c * CHUNK, CHUNK)
        kt_ref[:, pl.ds(r0, CHUNK)] = conv_chunk(c, wk).T.astype(BF16)
        return carry

    lax.fori_loop(0, nc, k_body, 0)

    lane2 = lax.broadcasted_iota(jnp.int32, (seq, dh), 1)
    va_ref[:, 0:dh] = v_ref[0]
    va_ref[:, dh:2 * dh] = jnp.where(lane2 == 0, 1.0, 0.0).astype(BF16)

    act = gr_ref[0, 0] + jnp.concatenate([gb_ref[0]] * nc, axis=0)
    kind = lax.broadcasted_iota(jnp.int32, act.shape, 0) % 8
    logsig = jnp.minimum(act, 0.0) - jnp.log1p(jnp.exp(-jnp.abs(act)))
    lf = jnp.where((kind == 1) | (kind == 3), logsig, 0.0)
    ui = lax.broadcasted_iota(jnp.int32, (CHUNK, CHUNK), 0)
    si = lax.broadcasted_iota(jnp.int32, (CHUNK, CHUNK), 1)
    act_ref[...] = act
    pre_ref[...] = _split3_dot(lf, jnp.where(ui <= si, 1.0, 0.0).astype(BF16))
    suf_ref[...] = _split3_dot(lf, jnp.where(ui >= si, 1.0, 0.0).astype(BF16))

    cf_ref[...] = jnp.zeros_like(cf_ref)
    cb_ref[...] = jnp.zeros_like(cb_ref)

    def step(c, c_ref, m, fwd):
        r0 = pl.multiple_of(c * CHUNK, CHUNK)
        g0 = pl.multiple_of(c * 8, 8)
        qc = qs_ref[pl.ds(r0, CHUNK), :]
        ktc = kt_ref[:, pl.ds(r0, CHUNK)]
        vc = va_ref[pl.ds(r0, CHUNK), :]
        rows = act_ref[pl.ds(g0, 8), :]
        if fwd:
            li = rows[0:1, :]
            b = pre_ref[pl.ds(g0, 8), :][1:2, :]
            total = b[:, CHUNK - 1:CHUNK]
            mask = ui >= si
        else:
            li = rows[2:3, :]
            b = suf_ref[pl.ds(g0, 8), :][3:4, :]
            total = b[:, 0:1]
            mask = ui <= si
        bt = jnp.broadcast_to(b, (CHUNK, CHUNK)).T
        dmat = jnp.where(mask, bt + (li - b), -jnp.inf)
        bcol = bt[:, 0:1]
        m_t = jnp.maximum(bcol + m, jnp.max(dmat, axis=-1, keepdims=True))
        inter = jnp.exp(bcol + m - m_t)
        s = _dot(qc, ktc) * jnp.exp(dmat - m_t)
        nd = _dot(s.astype(BF16), vc) + inter * _dot(qc, c_ref[...].astype(BF16))
        den = nd[:, dh:dh + 1]
        h = nd[:, 0:dh] / jnp.maximum(jnp.abs(den), jnp.exp(-m_t))
        w_log = total - b + li
        m_new = jnp.maximum(total + m, jnp.max(w_log, axis=-1, keepdims=True))
        decay = jnp.exp(total + m - m_new)
        kw = (ktc.astype(F32) * jnp.exp(w_log - m_new)).astype(BF16)
        c_ref[...] = decay * c_ref[...] + _dot(kw, vc)
        return h, m_new

    def scan_body(c, carry):
        m_f, m_b = carry
        h_f, m_f = step(c, cf_ref, m_f, True)
        hf_ref[pl.ds(pl.multiple_of(c * CHUNK, CHUNK), CHUNK), :] = h_f
        cb = nc - 1 - c
        h_b, m_b = step(cb, cb_ref, m_b, False)
        hb_ref[pl.ds(pl.multiple_of(cb * CHUNK, CHUNK), CHUNK), :] = h_b
        return m_f, m_b

    lax.fori_loop(0, nc, scan_body, (jnp.zeros((1, 1), F32), jnp.zeros((1, 1), F32)))

    def out_body(c, carry):
        rs = pl.ds(pl.multiple_of(c * CHUNK, CHUNK), CHUNK)
        y = _rms(hf_ref[rs, :] + hb_ref[rs, :], hg_ref[...])
        o_ref[0, rs, :] = (y * _sigmoid(oa_ref[0, rs, :])).astype(BF16)
        return carry

    lax.fori_loop(0, nc, out_body, 0)


def _mlstm(qk, va, oa, gr, gb, conv_w, head_g):
    bsz, seq, _ = qk.shape
    nc = seq // CHUNK
    dh = A_HEAD_DIM
    col = lambda off: pl.BlockSpec((1, seq, dh), lambda b, h: (b, 0, h + off))
    return pl.pallas_call(
        _mlstm_kernel,
        grid=(bsz, A_HEADS),
        in_specs=[col(0), col(A_HEADS), col(0), col(0),
                  pl.BlockSpec((1, 1, 8 * nc, CHUNK), lambda b, h: (b, h, 0, 0)),
                  pl.BlockSpec((1, 8, 1), lambda b, h: (h, 0, 0)),
                  pl.BlockSpec((CONV_WIDTH, dh), lambda b, h: (0, h)),
                  pl.BlockSpec((CONV_WIDTH, dh), lambda b, h: (0, h + A_HEADS)),
                  pl.BlockSpec((1, dh), lambda b, h: (0, h))],
        out_specs=col(0),
        out_shape=jax.ShapeDtypeStruct((bsz, seq, A_WIDTH), BF16),
        scratch_shapes=[pltpu.VMEM((seq + 16, dh), F32),
                        pltpu.VMEM((seq, dh), BF16),
                        pltpu.VMEM((dh, seq), BF16),
                        pltpu.VMEM((seq, 2 * dh), BF16),
                        pltpu.VMEM((8 * nc, CHUNK), F32),
                        pltpu.VMEM((8 * nc, CHUNK), F32),
                        pltpu.VMEM((8 * nc, CHUNK), F32),
                        pltpu.VMEM((seq, dh), F32),
                        pltpu.VMEM((seq, dh), F32),
                        pltpu.VMEM((dh, 2 * dh), F32),
                        pltpu.VMEM((dh, 2 * dh), F32)],
        compiler_params=_params(2),
        name="mlstm",
    )(qk, qk, va, oa, gr, gb, conv_w, conv_w, head_g)


def _rope_table_kernel(pos_ref, freq_ref, cos_ref, sin_ref, nsin_ref):
    ang = pos_ref[...].astype(F32) * freq_ref[...]
    sn = jnp.sin(ang)
    cos_ref[...] = jnp.cos(ang)
    sin_ref[...] = sn
    nsin_ref[...] = -sn


def _rope_tables(pos_rep, freq_rep):
    rows = pos_rep.shape[0]
    tr = min(rows, 512)
    spec = pl.BlockSpec((tr, LANES), lambda i: (i, 0))
    out = jax.ShapeDtypeStruct((rows, LANES), F32)
    return pl.pallas_call(
        _rope_table_kernel,
        grid=(rows // tr,),
        in_specs=[spec, _const_spec(freq_rep.shape)],
        out_specs=[spec, spec, spec],
        out_shape=[out, out, out],
        compiler_params=_params(1),
        name="rope_tables",
    )(pos_rep, freq_rep)


def _mla_in_kernel(x_ref, g_ref, w1_ref, qg_ref, wab_ref, kvg_ref, wk_ref, wv_ref, ct_ref, st_ref,
                   q_ref, k_ref, v_ref):
    hp = HEAD_PAD
    h = _rms(x_ref[...], g_ref[...]).astype(BF16)
    p1 = _dot(h, w1_ref[...])
    ct = ct_ref[...]
    st = st_ref[...]
    k_rope = p1[:, Q_LORA + KV_LORA:Q_LORA + KV_LORA + hp] * ct + p1[:, Q_LORA + KV_LORA + hp:] * st

    cq = _rms(p1[:, :Q_LORA], qg_ref[...]).astype(BF16)
    ab = _dot(cq, wab_ref[...])
    scale = (C_NOPE + C_ROPE) ** -0.5
    for hd in range(C_HEADS):
        a = ab[:, hd * hp:(hd + 1) * hp]
        b = ab[:, (C_HEADS + hd) * hp:(C_HEADS + hd + 1) * hp]
        q_ref[:, hd * hp:(hd + 1) * hp] = ((a * ct + b * st) * scale).astype(BF16)

    ckv = _rms(p1[:, Q_LORA:Q_LORA + KV_LORA], kvg_ref[...]).astype(BF16)
    kn = _dot(ckv, wk_ref[...])
    for hd in range(C_HEADS):
        k_ref[:, hd * hp:(hd + 1) * hp] = (kn[:, hd * hp:(hd + 1) * hp] + k_rope).astype(BF16)
    v_ref[...] = _dot(ckv, wv_ref[...]).astype(BF16)


def _mla_in(x2, g, w1, qg, wab, kvg, wk, wv, ct, st, tm=512):
    n = x2.shape[0]
    row = lambda w: pl.BlockSpec((tm, w), lambda i: (i, 0))
    return pl.pallas_call(
        _mla_in_kernel,
        grid=(n // tm,),
        in_specs=[row(D_MODEL), _const_spec(g.shape), _const_spec(w1.shape), _const_spec(qg.shape),
                  _const_spec(wab.shape), _const_spec(kvg.shape), _const_spec(wk.shape),
                  _const_spec(wv.shape), row(HEAD_PAD), row(HEAD_PAD)],
        out_specs=[row(C_HEADS * HEAD_PAD), row(C_HEADS * HEAD_PAD), row(C_HEADS * C_V)],
        out_shape=[jax.ShapeDtypeStruct((n, C_HEADS * HEAD_PAD), BF16),
                   jax.ShapeDtypeStruct((n, C_HEADS * HEAD_PAD), BF16),
                   jax.ShapeDtypeStruct((n, C_HEADS * C_V), BF16)],
        compiler_params=_params(1),
        name="mla_in",
    )(x2, g, w1, qg, wab, kvg, wk, wv, ct, st)


def _mla_attn_kernel(q_ref, k_ref, v_ref, o_ref, *, tq):
    seq = q_ref.shape[1]
    hp = HEAD_PAD
    lane = lax.broadcasted_iota(jnp.int32, (tq, 2 * C_V), 1)

    def body(i, carry):
        rs = pl.ds(pl.multiple_of(i * tq, tq), tq)
        outs = []
        for hh in range(2):
            q = q_ref[0, rs, hh * hp:(hh + 1) * hp]
            k = k_ref[0, :, hh * hp:(hh + 1) * hp]
            s = lax.dot_general(q, k, (((1,), (1,)), ((), ())), preferred_element_type=F32)
            p = jnp.exp(s - jnp.max(s, axis=-1, keepdims=True))
            l = jnp.sum(p, axis=-1, keepdims=True)
            outs.append(_dot(p.astype(BF16), v_ref[0]) / l)
        o_ref[0, rs, :] = jnp.where(lane < C_V, outs[0], outs[1]).astype(BF16)
        return carry

    lax.fori_loop(0, seq // tq, body, 0)


def _mla_attn(q, k, v, tq=256):
    bsz, seq, _ = q.shape
    qk_spec = pl.BlockSpec((1, seq, 2 * HEAD_PAD), lambda b, h: (b, 0, h))
    v_spec = pl.BlockSpec((1, seq, 2 * C_V), lambda b, h: (b, 0, h))
    return pl.pallas_call(
        functools.partial(_mla_attn_kernel, tq=tq),
        grid=(bsz, C_HEADS // 2),
        in_specs=[qk_spec, qk_spec, v_spec],
        out_specs=v_spec,
        out_shape=jax.ShapeDtypeStruct((bsz, seq, C_HEADS * C_V), BF16),
        compiler_params=_params(2),
        name="mla_attn",
    )(q, k, v)


def _mix_ffn_kernel(*refs, n_mix, tf, final):
    x_ref = refs[0]
    mix_refs = refs[1:1 + n_mix]
    wmix_ref, g_ref, w1_ref, w2_ref = refs[1 + n_mix:5 + n_mix]
    fg_ref = refs[5 + n_mix] if final else None
    o_ref = refs[-1]

    mix = jnp.concatenate([a_ref[...] for a_ref in mix_refs], axis=1)
    x = x_ref[...] + _dot(mix, wmix_ref[...])
    h = _rms(x, g_ref[...]).astype(BF16)
    for j in range(D_FF // tf):
        a = _dot(h, w1_ref[:, j * tf:(j + 1) * tf])
        a = jnp.square(jnp.maximum(a, 0.0)).astype(BF16)
        x = x + _dot(a, w2_ref[j * tf:(j + 1) * tf, :])
    if final:
        x = _rms(x, fg_ref[...])
    o_ref[...] = x


def _mix_ffn(x2, mixes, wmix, g, w1, w2, final_g=None, tm=512, tf=512):
    n = x2.shape[0]
    row = lambda w: pl.BlockSpec((tm, w), lambda i: (i, 0))
    final = final_g is not None
    args = [x2, *mixes, wmix, g, w1, w2] + ([final_g] if final else [])
    in_specs = ([row(D_MODEL)] + [row(a.shape[1]) for a in mixes]
                + [_const_spec(a.shape) for a in args[1 + len(mixes):]])
    return pl.pallas_call(
        functools.partial(_mix_ffn_kernel, n_mix=len(mixes), tf=tf, final=final),
        grid=(n // tm,),
        in_specs=in_specs,
        out_specs=row(D_MODEL),
        out_shape=jax.ShapeDtypeStruct((n, D_MODEL), F32),
        compiler_params=_params(1),
        name="mix_ffn",
    )(*args)


def _pad_heads(w, width, parts):
    k = w.shape[0]
    w = w.reshape(k, C_HEADS, width)
    cols = [w[:, :, a:b] if a is not None else jnp.zeros((k, C_HEADS, b), w.dtype) for a, b in parts]
    used = sum(c.shape[2] for c in cols)
    cols.append(jnp.zeros((k, C_HEADS, HEAD_PAD - used), w.dtype))
    return jnp.concatenate(cols, axis=2).reshape(k, C_HEADS * HEAD_PAD)


def kernel(x, positions, ab_norm, ab_w_in, ab_conv, ab_gate_b, ab_head_g, ab_v_g, ab_ws, ab_bs,
           ab_w_out, c_norm, c_w_in, c_q_g, c_kv_g, c_w_uq, c_w_ukv, c_w_out, ffn_norm, ffn_w1,
           ffn_w2, final_norm):
    bsz, seq, _ = x.shape
    n = bsz * seq
    nc = seq // CHUNK
    half = C_ROPE // 2
    x2 = x.reshape(n, D_MODEL)

    freq = ROPE_BASE ** (-jnp.arange(half, dtype=F32) / half)
    pos_rep = jnp.repeat(positions.reshape(n), half).reshape(n * half // LANES, LANES)
    freq_rep = jnp.tile(freq, LANES // half).reshape(1, LANES)
    cos, sin, nsin = (t.reshape(n, half) for t in _rope_tables(pos_rep, freq_rep))
    ones = lambda w: jnp.ones((n, w), F32)
    zeros = lambda w: jnp.zeros((n, w), F32)
    cos_t = jnp.concatenate([ones(C_NOPE), cos, cos, ones(HEAD_PAD - C_NOPE - C_ROPE)], axis=1)
    sin_t = jnp.concatenate([zeros(C_NOPE), nsin, sin, zeros(HEAD_PAD - C_NOPE - C_ROPE)], axis=1)

    for layer in range(DEPTH):
        j = layer // 2
        final_g = final_norm.reshape(1, D_MODEL) if layer == DEPTH - 1 else None
        ffn_args = (ffn_norm[layer].reshape(1, D_MODEL), ffn_w1[layer].astype(BF16),
                    ffn_w2[layer].astype(BF16))
        if layer % 2 == 0:
            w = ab_w_in[j].astype(BF16)
            o0 = 4 * A_WIDTH
            o1 = o0 + 4 * A_HEADS
            qk, va, oa, gates, hb = _ab_in(
                x2, ab_norm[j].reshape(1, D_MODEL), w[:, :2 * A_WIDTH], w[:, 2 * A_WIDTH:3 * A_WIDTH],
                w[:, 3 * A_WIDTH:o0], w[:, o0:o1], w[:, o1:], ab_v_g[j].reshape(1, B_WIDTH),
                ab_ws[j].astype(BF16), ab_bs[j].T)
            gr = gates.reshape(bsz, nc, CHUNK, 4, A_HEADS).transpose(0, 4, 1, 3, 2)
            gr = jnp.pad(gr, ((0, 0), (0, 0), (0, 0), (0, 4), (0, 0))).reshape(bsz, A_HEADS, 8 * nc, CHUNK)
            gb = jnp.pad(ab_gate_b[j].reshape(4, A_HEADS).T, ((0, 0), (0, 4))).reshape(A_HEADS, 8, 1)
            ha = _mlstm(qk.reshape(bsz, seq, 2 * A_WIDTH), va.reshape(bsz, seq, A_WIDTH),
                        oa.reshape(bsz, seq, A_WIDTH), gr, gb, ab_conv[j],
                        ab_head_g[j].reshape(1, A_WIDTH))
            x2 = _mix_ffn(x2, [ha.reshape(n, A_WIDTH), hb], ab_w_out[j].astype(BF16),
                          *ffn_args, final_g=final_g)
        else:
            w_in = c_w_in[j]
            kr1 = w_in[:, Q_LORA + KV_LORA:Q_LORA + KV_LORA + half]
            kr2 = w_in[:, Q_LORA + KV_LORA + half:]
            zc = lambda w: jnp.zeros((D_MODEL, w), w_in.dtype)
            tail = HEAD_PAD - C_NOPE - C_ROPE
            w1 = jnp.concatenate([w_in[:, :Q_LORA + KV_LORA],
                                  zc(C_NOPE), kr1, kr2, zc(tail),
                                  zc(C_NOPE), kr2, kr1, zc(tail)], axis=1).astype(BF16)
            qw = C_NOPE + C_ROPE
            wa = _pad_heads(c_w_uq[j], qw, [(0, qw)])
            wb = _pad_heads(c_w_uq[j], qw, [(None, C_NOPE), (C_NOPE + half, qw), (C_NOPE, C_NOPE + half)])
            wab = jnp.concatenate([wa, wb], axis=1).astype(BF16)
            wk = _pad_heads(c_w_ukv[j], C_NOPE + C_V, [(0, C_NOPE)]).astype(BF16)
            wv = c_w_ukv[j].reshape(KV_LORA, C_HEADS, C_NOPE + C_V)[:, :, C_NOPE:]
            wv = wv.reshape(KV_LORA, C_HEADS * C_V).astype(BF16)
            q, k, v = _mla_in(x2, c_norm[j].reshape(1, D_MODEL), w1, c_q_g[j].reshape(1, Q_LORA), wab,
                              c_kv_g[j].reshape(1, KV_LORA), wk, wv, cos_t, sin_t)
            o = _mla_attn(q.reshape(bsz, seq, -1), k.reshape(bsz, seq, -1), v.reshape(bsz, seq, -1))
            x2 = _mix_ffn(x2, [o.reshape(n, C_HEADS * C_V)], c_w_out[j].astype(BF16),
                          *ffn_args, final_g=final_g)
    return x2.reshape(bsz, seq, D_MODEL)
```

```python
import functools

import jax
import jax.numpy as jnp
from jax import lax
from jax.experimental import pallas as pl
from jax.experimental.pallas import tpu as pltpu

F32 = jnp.float32
BF16 = jnp.bfloat16

D_MODEL = 1024
DEPTH = 4
A_WIDTH = 512
A_HEADS = 4
A_HEAD_DIM = 128
CHUNK = 128
CONV_WIDTH = 5
B_WIDTH = 512
B_GROUPS = 4
B_GROUP_DIM = 128
C_HEADS = 16
C_NOPE = 64
C_ROPE = 32
C_V = 64
Q_LORA = 384
KV_LORA = 256
ROPE_BASE = 10000.0
D_FF = 4 * D_MODEL
EPS = 1e-6

LANES = 128
HEAD_PAD = 128
V_EXT_ROWS = C_V + 16
LOG2E = 1.4426950408889634
VMEM_LIMIT = 56 * 1024 * 1024


def _params(n_axes):
    return pltpu.CompilerParams(
        dimension_semantics=("arbitrary",) * n_axes, vmem_limit_bytes=VMEM_LIMIT)


def _const_spec(shape):
    nd = len(shape)
    return pl.BlockSpec(shape, lambda *_: (0,) * nd, pipeline_mode=pl.Buffered(1))


def _rms(x, g):
    return x * lax.rsqrt(jnp.mean(x * x, axis=-1, keepdims=True) + EPS) * g


def _dot(a, b):
    return jnp.dot(a, b, preferred_element_type=F32)


def _sigmoid(x):
    return 1.0 / (1.0 + jnp.exp(-x))


def _ab_in_kernel(x_ref, g_ref, wqk_ref, wv_ref, wo_ref, wg_ref, wuv_ref, vg_ref, ws_ref, bs_ref,
                  qk_ref, va_ref, oa_ref, gt_ref, hb_ref):
    tm = x_ref.shape[0]
    h = _rms(x_ref[...], g_ref[...]).astype(BF16)
    qk_ref[...] = _dot(h, wqk_ref[...])
    va_ref[...] = _dot(h, wv_ref[...]).astype(BF16)
    oa_ref[...] = _dot(h, wo_ref[...])
    gt_ref[...] = _dot(h, wg_ref[...])

    uv = _dot(h, wuv_ref[...])
    uv = 0.5 * uv * (1.0 + jnp.tanh(0.7978845608028654 * (uv + 0.044715 * (uv * uv * uv))))
    n_chunks = tm // CHUNK
    for g in range(B_GROUPS):
        lo = g * B_GROUP_DIM
        u = uv[:, lo:lo + B_GROUP_DIM]
        vb = uv[:, B_WIDTH + lo:B_WIDTH + lo + B_GROUP_DIM]
        vb = _rms(vb, vg_ref[:, lo:lo + B_GROUP_DIM]).astype(BF16)
        rhs = jnp.concatenate([vb[c * CHUNK:(c + 1) * CHUNK, :] for c in range(n_chunks)], axis=1)
        sp = _dot(ws_ref[g], rhs) + bs_ref[:, g:g + 1]
        for c in range(n_chunks):
            hb_ref[c * CHUNK:(c + 1) * CHUNK, lo:lo + B_GROUP_DIM] = (
                u[c * CHUNK:(c + 1) * CHUNK, :] * sp[:, c * B_GROUP_DIM:(c + 1) * B_GROUP_DIM]
            ).astype(BF16)


def _ab_in(x2, g, wqk, wv, wo, wg, wuv, vg, ws, bs_t, tm=512):
    n = x2.shape[0]
    row = lambda w: pl.BlockSpec((tm, w), lambda i: (i, 0))
    return pl.pallas_call(
        _ab_in_kernel,
        grid=(n // tm,),
        in_specs=[row(D_MODEL), _const_spec(g.shape), _const_spec(wqk.shape), _const_spec(wv.shape),
                  _const_spec(wo.shape), _const_spec(wg.shape), _const_spec(wuv.shape),
                  _const_spec(vg.shape), _const_spec(ws.shape), _const_spec(bs_t.shape)],
        out_specs=[row(2 * A_WIDTH), row(A_WIDTH), row(A_WIDTH), row(4 * A_HEADS), row(B_WIDTH)],
        out_shape=[jax.ShapeDtypeStruct((n, 2 * A_WIDTH), F32),
                   jax.ShapeDtypeStruct((n, A_WIDTH), BF16),
                   jax.ShapeDtypeStruct((n, A_WIDTH), F32),
                   jax.ShapeDtypeStruct((n, 4 * A_HEADS), F32),
                   jax.ShapeDtypeStruct((n, B_WIDTH), BF16)],
        compiler_params=_params(1),
        name="ab_in",
    )(x2, g, wqk, wv, wo, wg, wuv, vg, ws, bs_t)


def _split3_dot(x, tri):
    x0 = x.astype(BF16)
    r1 = x - x0.astype(F32)
    x1 = r1.astype(BF16)
    x2 = (r1 - x1.astype(F32)).astype(BF16)
    return _dot(x0, tri) + _dot(x1, tri) + _dot(x2, tri)


def _mlstm_kernel(q_ref, k_ref, v_ref, oa_ref, gr_ref, gb_ref, cwq_ref, cwk_ref, hg_ref, o_ref,
                  xp_ref, qs_ref, kt_ref, va_ref, act_ref, pre_ref, suf_ref, hf_ref, hb_ref,
                  cf_ref, cb_ref):
    seq = q_ref.shape[1]
    nc = seq // CHUNK
    dh = A_HEAD_DIM
    halo = 8
    pad = (CONV_WIDTH - 1) // 2

    zeros_halo = jnp.zeros((halo, dh), F32)
    xp_ref[0:halo, :] = zeros_halo
    xp_ref[halo + seq:halo + seq + halo, :] = zeros_halo

    def conv_chunk(c, w):
        r0 = pl.multiple_of(c * CHUNK, CHUNK)
        win = xp_ref[pl.ds(r0, CHUNK + 2 * halo), :]
        y = win[halo - pad:halo - pad + CHUNK, :] * w[0:1, :]
        for j in range(1, CONV_WIDTH):
            y = y + win[halo - pad + j:halo - pad + j + CHUNK, :] * w[j:j + 1, :]
        return y * _sigmoid(y)

    xp_ref[halo:halo + seq, :] = q_ref[0]
    wq = cwq_ref[...]

    def q_body(c, carry):
        r0 = pl.multiple_of(c * CHUNK, CHUNK)
        qs_ref[pl.ds(r0, CHUNK), :] = (conv_chunk(c, wq) * (dh ** -0.5)).astype(BF16)
        return carry

    lax.fori_loop(0, nc, q_body, 0)

    xp_ref[halo:halo + seq, :] = k_ref[0]
    wk = cwk_ref[...]

    def k_body(c, carry):
        r0 = pl.multiple_of(c * CHUNK, CHUNK)
        kt_ref[:, pl.ds(r0, CHUNK)] = conv_chunk(c, wk).T.astype(BF16)
        return carry

    lax.fori_loop(0, nc, k_body, 0)

    lane2 = lax.broadcasted_iota(jnp.int32, (seq, dh), 1)
    va_ref[:, 0:dh] = v_ref[0]
    va_ref[:, dh:2 * dh] = jnp.where(lane2 == 0, 1.0, 0.0).astype(BF16)

    act = gr_ref[0, 0] + jnp.concatenate([gb_ref[0]] * nc, axis=0)
    kind = lax.broadcasted_iota(jnp.int32, act.shape, 0) % 8
    logsig = jnp.minimum(act, 0.0) - jnp.log1p(jnp.exp(-jnp.abs(act)))
    lf = jnp.where((kind == 1) | (kind == 3), logsig, 0.0)
    ui = lax.broadcasted_iota(jnp.int32, (CHUNK, CHUNK), 0)
    si = lax.broadcasted_iota(jnp.int32, (CHUNK, CHUNK), 1)
    act_ref[...] = act
    pre_ref[...] = _split3_dot(lf, jnp.where(ui <= si, 1.0, 0.0).astype(BF16))
    suf_ref[...] = _split3_dot(lf, jnp.where(ui >= si, 1.0, 0.0).astype(BF16))

    cf_ref[...] = jnp.zeros_like(cf_ref)
    cb_ref[...] = jnp.zeros_like(cb_ref)

    def step(c, c_ref, m, fwd):
        r0 = pl.multiple_of(c * CHUNK, CHUNK)
        g0 = pl.multiple_of(c * 8, 8)
        qc = qs_ref[pl.ds(r0, CHUNK), :]
        ktc = kt_ref[:, pl.ds(r0, CHUNK)]
        vc = va_ref[pl.ds(r0, CHUNK), :]
        rows = act_ref[pl.ds(g0, 8), :]
        if fwd:
            li = rows[0:1, :]
            b = pre_ref[pl.ds(g0, 8), :][1:2, :]
            total = b[:, CHUNK - 1:CHUNK]
            mask = ui >= si
        else:
            li = rows[2:3, :]
            b = suf_ref[pl.ds(g0, 8), :][3:4, :]
            total = b[:, 0:1]
            mask = ui <= si
        bt = jnp.broadcast_to(b, (CHUNK, CHUNK)).T
        dmat = jnp.where(mask, bt + (li - b), -jnp.inf)
        bcol = bt[:, 0:1]
        m_t = jnp.maximum(bcol + m, jnp.max(dmat, axis=-1, keepdims=True))
        inter = jnp.exp(bcol + m - m_t)
        s = _dot(qc, ktc) * jnp.exp(dmat - m_t)
        nd = _dot(s.astype(BF16), vc) + inter * _dot(qc, c_ref[...].astype(BF16))
        den = nd[:, dh:dh + 1]
        h = nd[:, 0:dh] / jnp.maximum(jnp.abs(den), jnp.exp(-m_t))
        w_log = total - b + li
        m_new = jnp.maximum(total + m, jnp.max(w_log, axis=-1, keepdims=True))
        decay = jnp.exp(total + m - m_new)
        kw = (ktc.astype(F32) * jnp.exp(w_log - m_new)).astype(BF16)
        c_ref[...] = decay * c_ref[...] + _dot(kw, vc)
        return h, m_new

    def scan_body(c, carry):
        m_f, m_b = carry
        h_f, m_f = step(c, cf_ref, m_f, True)
        hf_ref[pl.ds(pl.multiple_of(c * CHUNK, CHUNK), CHUNK), :] = h_f
        cb = nc - 1 - c
        h_b, m_b = step(cb, cb_ref, m_b, False)
        hb_ref[pl.ds(pl.multiple_of(cb * CHUNK, CHUNK), CHUNK), :] = h_b
        return m_f, m_b

    lax.fori_loop(0, nc, scan_body, (jnp.zeros((1, 1), F32), jnp.zeros((1, 1), F32)))

    def out_body(c, carry):
        rs = pl.ds(pl.multiple_of(c * CHUNK, CHUNK), CHUNK)
        y = _rms(hf_ref[rs, :] + hb_ref[rs, :], hg_ref[...])
        o_ref[0, rs, :] = (y * _sigmoid(oa_ref[0, rs, :])).astype(BF16)
        return carry

    lax.fori_loop(0, nc, out_body, 0)


def _mlstm(qk, va, oa, gr, gb, conv_w, head_g):
    bsz, seq, _ = qk.shape
    nc = seq // CHUNK
    dh = A_HEAD_DIM
    col = lambda off: pl.BlockSpec((1, seq, dh), lambda b, h: (b, 0, h + off))
    return pl.pallas_call(
        _mlstm_kernel,
        grid=(bsz, A_HEADS),
        in_specs=[col(0), col(A_HEADS), col(0), col(0),
                  pl.BlockSpec((1, 1, 8 * nc, CHUNK), lambda b, h: (b, h, 0, 0)),
                  pl.BlockSpec((1, 8, 1), lambda b, h: (h, 0, 0)),
                  pl.BlockSpec((CONV_WIDTH, dh), lambda b, h: (0, h)),
                  pl.BlockSpec((CONV_WIDTH, dh), lambda b, h: (0, h + A_HEADS)),
                  pl.BlockSpec((1, dh), lambda b, h: (0, h))],
        out_specs=col(0),
        out_shape=jax.ShapeDtypeStruct((bsz, seq, A_WIDTH), BF16),
        scratch_shapes=[pltpu.VMEM((seq + 16, dh), F32),
                        pltpu.VMEM((seq, dh), BF16),
                        pltpu.VMEM((dh, seq), BF16),
                        pltpu.VMEM((seq, 2 * dh), BF16),
                        pltpu.VMEM((8 * nc, CHUNK), F32),
                        pltpu.VMEM((8 * nc, CHUNK), F32),
                        pltpu.VMEM((8 * nc, CHUNK), F32),
                        pltpu.VMEM((seq, dh), F32),
                        pltpu.VMEM((seq, dh), F32),
                        pltpu.VMEM((dh, 2 * dh), F32),
                        pltpu.VMEM((dh, 2 * dh), F32)],
        compiler_params=_params(2),
        name="mlstm",
    )(qk, qk, va, oa, gr, gb, conv_w, conv_w, head_g)


def _rope_table_kernel(pos_ref, freq_ref, cos_ref, sin_ref, nsin_ref):
    ang = pos_ref[...].astype(F32) * freq_ref[...]
    sn = jnp.sin(ang)
    cos_ref[...] = jnp.cos(ang)
    sin_ref[...] = sn
    nsin_ref[...] = -sn


def _rope_tables(pos_rep, freq_rep):
    rows = pos_rep.shape[0]
    tr = min(rows, 512)
    spec = pl.BlockSpec((tr, LANES), lambda i: (i, 0))
    out = jax.ShapeDtypeStruct((rows, LANES), F32)
    return pl.pallas_call(
        _rope_table_kernel,
        grid=(rows // tr,),
        in_specs=[spec, _const_spec(freq_rep.shape)],
        out_specs=[spec, spec, spec],
        out_shape=[out, out, out],
        compiler_params=_params(1),
        name="rope_tables",
    )(pos_rep, freq_rep)


def _mla_in_kernel(x_ref, g_ref, w1_ref, qg_ref, wab_ref, kvg_ref, wk_ref, wv_ref, ct_ref, st_ref,
                   q_ref, k_ref, v_ref):
    hp = HEAD_PAD
    h = _rms(x_ref[...], g_ref[...]).astype(BF16)
    p1 = _dot(h, w1_ref[...])
    ct = ct_ref[...]
    st = st_ref[...]
    k_rope = p1[:, Q_LORA + KV_LORA:Q_LORA + KV_LORA + hp] * ct + p1[:, Q_LORA + KV_LORA + hp:] * st

    cq = _rms(p1[:, :Q_LORA], qg_ref[...]).astype(BF16)
    ab = _dot(cq, wab_ref[...])
    scale = (C_NOPE + C_ROPE) ** -0.5 * LOG2E
    for hd in range(C_HEADS):
        a = ab[:, hd * hp:(hd + 1) * hp]
        b = ab[:, (C_HEADS + hd) * hp:(C_HEADS + hd + 1) * hp]
        q_ref[:, hd * hp:(hd + 1) * hp] = ((a * ct + b * st) * scale).astype(BF16)

    ckv = _rms(p1[:, Q_LORA:Q_LORA + KV_LORA], kvg_ref[...]).astype(BF16)
    kn = _dot(ckv, wk_ref[...])
    for hd in range(C_HEADS):
        k_ref[:, hd * hp:(hd + 1) * hp] = (kn[:, hd * hp:(hd + 1) * hp] + k_rope).astype(BF16)
    v_ref[...] = lax.dot_general(wv_ref[...], ckv, (((1,), (1,)), ((), ())),
                                 preferred_element_type=F32).astype(BF16)


def _mla_in(x2, g, w1, qg, wab, kvg, wk, wv, ct, st, tm=512):
    n = x2.shape[0]
    row = lambda w: pl.BlockSpec((tm, w), lambda i: (i, 0))
    return pl.pallas_call(
        _mla_in_kernel,
        grid=(n // tm,),
        in_specs=[row(D_MODEL), _const_spec(g.shape), _const_spec(w1.shape), _const_spec(qg.shape),
                  _const_spec(wab.shape), _const_spec(kvg.shape), _const_spec(wk.shape),
                  _const_spec(wv.shape), row(HEAD_PAD), row(HEAD_PAD)],
        out_specs=[row(C_HEADS * HEAD_PAD), row(C_HEADS * HEAD_PAD),
                   pl.BlockSpec((C_HEADS * C_V, tm), lambda i: (0, i))],
        out_shape=[jax.ShapeDtypeStruct((n, C_HEADS * HEAD_PAD), BF16),
                   jax.ShapeDtypeStruct((n, C_HEADS * HEAD_PAD), BF16),
                   jax.ShapeDtypeStruct((C_HEADS * C_V, n), BF16)],
        compiler_params=_params(1),
        name="mla_in",
    )(x2, g, w1, qg, wab, kvg, wk, wv, ct, st)


def _mla_attn_kernel(q_ref, k_ref, vt_ref, o_ref, vx_ref, s_ref, *, tq):
    seq = q_ref.shape[1]
    hp = HEAD_PAD
    n_tiles = seq // tq
    ones_rows = V_EXT_ROWS - C_V
    for hh in range(2):
        vx_ref[hh, 0:C_V, :] = vt_ref[hh * C_V:(hh + 1) * C_V, :]
        vx_ref[hh, C_V:V_EXT_ROWS, :] = jnp.ones((ones_rows, seq), BF16)

    def rows(t):
        return pl.ds(pl.multiple_of(t * tq, tq), tq)

    def scores(t):
        maxima = []
        for hh in range(2):
            q = q_ref[0, rows(t), hh * hp:(hh + 1) * hp]
            k = k_ref[0, :, hh * hp:(hh + 1) * hp]
            st = lax.dot_general(k, q, (((1,), (1,)), ((), ())), preferred_element_type=F32)
            s_ref[t % 2, hh] = st
            maxima.append(jnp.max(st, axis=0, keepdims=True))
        return tuple(maxima)

    def values(t, maxima):
        outs = []
        for hh in range(2):
            pt = jnp.exp2(s_ref[t % 2, hh] - maxima[hh]).astype(BF16)
            ot = _dot(vx_ref[hh], pt)
            outs.append(ot[0:C_V, :] / ot[C_V:C_V + 1, :])
        o_ref[0, rows(t), :] = jnp.concatenate(outs, axis=0).T.astype(BF16)

    def tile_body(t, m_prev):
        values(t - 1, m_prev)
        return scores(t)

    m_last = lax.fori_loop(1, n_tiles, tile_body, scores(0))
    values(n_tiles - 1, m_last)


def _mla_attn(q, k, vt, tq=256):
    bsz, seq, _ = q.shape
    qk_spec = pl.BlockSpec((1, seq, 2 * HEAD_PAD), lambda b, h: (b, 0, h))
    return pl.pallas_call(
        functools.partial(_mla_attn_kernel, tq=tq),
        grid=(bsz, C_HEADS // 2),
        in_specs=[qk_spec, qk_spec, pl.BlockSpec((2 * C_V, seq), lambda b, h: (h, b))],
        out_specs=pl.BlockSpec((1, seq, 2 * C_V), lambda b, h: (b, 0, h)),
        out_shape=jax.ShapeDtypeStruct((bsz, seq, C_HEADS * C_V), BF16),
        scratch_shapes=[pltpu.VMEM((2, V_EXT_ROWS, seq), BF16),
                        pltpu.VMEM((2, 2, seq, tq), F32)],
        compiler_params=_params(2),
        name="mla_attn",
    )(q, k, vt)


def _mix_ffn_kernel(*refs, n_mix, tf, final):
    x_ref = refs[0]
    mix_refs = refs[1:1 + n_mix]
    wmix_ref, g_ref, w1_ref, w2_ref = refs[1 + n_mix:5 + n_mix]
    fg_ref = refs[5 + n_mix] if final else None
    o_ref = refs[-1]

    mix = jnp.concatenate([a_ref[...] for a_ref in mix_refs], axis=1)
    x = x_ref[...] + _dot(mix, wmix_ref[...])
    h = _rms(x, g_ref[...]).astype(BF16)
    for j in range(D_FF // tf):
        a = _dot(h, w1_ref[:, j * tf:(j + 1) * tf])
        a = jnp.square(jnp.maximum(a, 0.0)).astype(BF16)
        x = x + _dot(a, w2_ref[j * tf:(j + 1) * tf, :])
    if final:
        x = _rms(x, fg_ref[...])
    o_ref[...] = x


def _mix_ffn(x2, mixes, wmix, g, w1, w2, final_g=None, tm=512, tf=512):
    n = x2.shape[0]
    row = lambda w: pl.BlockSpec((tm, w), lambda i: (i, 0))
    final = final_g is not None
    args = [x2, *mixes, wmix, g, w1, w2] + ([final_g] if final else [])
    in_specs = ([row(D_MODEL)] + [row(a.shape[1]) for a in mixes]
                + [_const_spec(a.shape) for a in args[1 + len(mixes):]])
    return pl.pallas_call(
        functools.partial(_mix_ffn_kernel, n_mix=len(mixes), tf=tf, final=final),
        grid=(n // tm,),
        in_specs=in_specs,
        out_specs=row(D_MODEL),
        out_shape=jax.ShapeDtypeStruct((n, D_MODEL), F32),
        compiler_params=_params(1),
        name="mix_ffn",
    )(*args)


def _pad_heads(w, width, parts):
    k = w.shape[0]
    w = w.reshape(k, C_HEADS, width)
    cols = [w[:, :, a:b] if a is not None else jnp.zeros((k, C_HEADS, b), w.dtype) for a, b in parts]
    used = sum(c.shape[2] for c in cols)
    cols.append(jnp.zeros((k, C_HEADS, HEAD_PAD - used), w.dtype))
    return jnp.concatenate(cols, axis=2).reshape(k, C_HEADS * HEAD_PAD)


def kernel(x, positions, ab_norm, ab_w_in, ab_conv, ab_gate_b, ab_head_g, ab_v_g, ab_ws, ab_bs,
           ab_w_out, c_norm, c_w_in, c_q_g, c_kv_g, c_w_uq, c_w_ukv, c_w_out, ffn_norm, ffn_w1,
           ffn_w2, final_norm):
    bsz, seq, _ = x.shape
    n = bsz * seq
    nc = seq // CHUNK
    half = C_ROPE // 2
    x2 = x.reshape(n, D_MODEL)

    freq = ROPE_BASE ** (-jnp.arange(half, dtype=F32) / half)
    pos_rep = jnp.repeat(positions.reshape(n), half).reshape(n * half // LANES, LANES)
    freq_rep = jnp.tile(freq, LANES // half).reshape(1, LANES)
    cos, sin, nsin = (t.reshape(n, half) for t in _rope_tables(pos_rep, freq_rep))
    ones = lambda w: jnp.ones((n, w), F32)
    zeros = lambda w: jnp.zeros((n, w), F32)
    cos_t = jnp.concatenate([ones(C_NOPE), cos, cos, ones(HEAD_PAD - C_NOPE - C_ROPE)], axis=1)
    sin_t = jnp.concatenate([zeros(C_NOPE), nsin, sin, zeros(HEAD_PAD - C_NOPE - C_ROPE)], axis=1)

    for layer in range(DEPTH):
        j = layer // 2
        final_g = final_norm.reshape(1, D_MODEL) if layer == DEPTH - 1 else None
        ffn_args = (ffn_norm[layer].reshape(1, D_MODEL), ffn_w1[layer].astype(BF16),
                    ffn_w2[layer].astype(BF16))
        if layer % 2 == 0:
            w = ab_w_in[j].astype(BF16)
            o0 = 4 * A_WIDTH
            o1 = o0 + 4 * A_HEADS
            qk, va, oa, gates, hb = _ab_in(
                x2, ab_norm[j].reshape(1, D_MODEL), w[:, :2 * A_WIDTH], w[:, 2 * A_WIDTH:3 * A_WIDTH],
                w[:, 3 * A_WIDTH:o0], w[:, o0:o1], w[:, o1:], ab_v_g[j].reshape(1, B_WIDTH),
                ab_ws[j].astype(BF16), ab_bs[j].T)
            gr = gates.reshape(bsz, nc, CHUNK, 4, A_HEADS).transpose(0, 4, 1, 3, 2)
            gr = jnp.pad(gr, ((0, 0), (0, 0), (0, 0), (0, 4), (0, 0))).reshape(bsz, A_HEADS, 8 * nc, CHUNK)
            gb = jnp.pad(ab_gate_b[j].reshape(4, A_HEADS).T, ((0, 0), (0, 4))).reshape(A_HEADS, 8, 1)
            ha = _mlstm(qk.reshape(bsz, seq, 2 * A_WIDTH), va.reshape(bsz, seq, A_WIDTH),
                        oa.reshape(bsz, seq, A_WIDTH), gr, gb, ab_conv[j],
                        ab_head_g[j].reshape(1, A_WIDTH))
            x2 = _mix_ffn(x2, [ha.reshape(n, A_WIDTH), hb], ab_w_out[j].astype(BF16),
                          *ffn_args, final_g=final_g)
        else:
            w_in = c_w_in[j]
            kr1 = w_in[:, Q_LORA + KV_LORA:Q_LORA + KV_LORA + half]
            kr2 = w_in[:, Q_LORA + KV_LORA + half:]
            zc = lambda w: jnp.zeros((D_MODEL, w), w_in.dtype)
            tail = HEAD_PAD - C_NOPE - C_ROPE
            w1 = jnp.concatenate([w_in[:, :Q_LORA + KV_LORA],
                                  zc(C_NOPE), kr1, kr2, zc(tail),
                                  zc(C_NOPE), kr2, kr1, zc(tail)], axis=1).astype(BF16)
            qw = C_NOPE + C_ROPE
            wa = _pad_heads(c_w_uq[j], qw, [(0, qw)])
            wb = _pad_heads(c_w_uq[j], qw, [(None, C_NOPE), (C_NOPE + half, qw), (C_NOPE, C_NOPE + half)])
            wab = jnp.concatenate([wa, wb], axis=1).astype(BF16)
            wk = _pad_heads(c_w_ukv[j], C_NOPE + C_V, [(0, C_NOPE)]).astype(BF16)
            wv = c_w_ukv[j].reshape(KV_LORA, C_HEADS, C_NOPE + C_V)[:, :, C_NOPE:]
            wv_t = wv.reshape(KV_LORA, C_HEADS * C_V).T.astype(BF16)
            q, k, vt = _mla_in(x2, c_norm[j].reshape(1, D_MODEL), w1, c_q_g[j].reshape(1, Q_LORA), wab,
                               c_kv_g[j].reshape(1, KV_LORA), wk, wv_t, cos_t, sin_t)
            o = _mla_attn(q.reshape(bsz, seq, -1), k.reshape(bsz, seq, -1), vt)
            x2 = _mix_ffn(x2, [o.reshape(n, C_HEADS * C_V)], c_w_out[j].astype(BF16),
                          *ffn_args, final_g=final_g)
    return x2.reshape(bsz, seq, D_MODEL)
```

```python
import functools

import jax
import jax.numpy as jnp
from jax import lax
from jax.experimental import pallas as pl
from jax.experimental.pallas import tpu as pltpu

F32 = jnp.float32
BF16 = jnp.bfloat16

D_MODEL = 1024
DEPTH = 4
A_WIDTH = 512
A_HEADS = 4
A_HEAD_DIM = 128
CHUNK = 128
CONV_WIDTH = 5
B_WIDTH = 512
B_GROUPS = 4
B_GROUP_DIM = 128
C_HEADS = 16
C_NOPE = 64
C_ROPE = 32
C_V = 64
Q_LORA = 384
KV_LORA = 256
ROPE_BASE = 10000.0
D_FF = 4 * D_MODEL
EPS = 1e-6

LANES = 128
HEAD_PAD = 128
V_EXT_ROWS = C_V + 16
LOG2E = 1.4426950408889634
VMEM_LIMIT = 56 * 1024 * 1024


def _params(n_axes):
    return pltpu.CompilerParams(
        dimension_semantics=("arbitrary",) * n_axes, vmem_limit_bytes=VMEM_LIMIT)


def _const_spec(shape):
    nd = len(shape)
    return pl.BlockSpec(shape, lambda *_: (0,) * nd, pipeline_mode=pl.Buffered(1))


def _rms(x, g):
    return x * lax.rsqrt(jnp.mean(x * x, axis=-1, keepdims=True) + EPS) * g


def _dot(a, b):
    return jnp.dot(a, b, preferred_element_type=F32)


def _sigmoid(x):
    return 1.0 / (1.0 + jnp.exp(-x))


def _ab_in_kernel(x_ref, g_ref, wqk_ref, wv_ref, wo_ref, wg_ref, wuv_ref, vg_ref, ws_ref, bs_ref,
                  qk_ref, va_ref, oa_ref, gt_ref, hb_ref):
    tm = x_ref.shape[0]
    h = _rms(x_ref[...], g_ref[...]).astype(BF16)
    qk_ref[...] = _dot(h, wqk_ref[...])
    va_ref[...] = _dot(h, wv_ref[...]).astype(BF16)
    oa_ref[...] = _dot(h, wo_ref[...])
    gt_ref[...] = _dot(h, wg_ref[...])

    uv = _dot(h, wuv_ref[...])
    uv = 0.5 * uv * (1.0 + jnp.tanh(0.7978845608028654 * (uv + 0.044715 * (uv * uv * uv))))
    n_chunks = tm // CHUNK
    for g in range(B_GROUPS):
        lo = g * B_GROUP_DIM
        u = uv[:, lo:lo + B_GROUP_DIM]
        vb = uv[:, B_WIDTH + lo:B_WIDTH + lo + B_GROUP_DIM]
        vb = _rms(vb, vg_ref[:, lo:lo + B_GROUP_DIM]).astype(BF16)
        rhs = jnp.concatenate([vb[c * CHUNK:(c + 1) * CHUNK, :] for c in range(n_chunks)], axis=1)
        sp = _dot(ws_ref[g], rhs) + bs_ref[:, g:g + 1]
        for c in range(n_chunks):
            hb_ref[c * CHUNK:(c + 1) * CHUNK, lo:lo + B_GROUP_DIM] = (
                u[c * CHUNK:(c + 1) * CHUNK, :] * sp[:, c * B_GROUP_DIM:(c + 1) * B_GROUP_DIM]
            ).astype(BF16)


def _ab_in(x2, g, wqk, wv, wo, wg, wuv, vg, ws, bs_t, tm=512):
    n = x2.shape[0]
    row = lambda w: pl.BlockSpec((tm, w), lambda i: (i, 0))
    return pl.pallas_call(
        _ab_in_kernel,
        grid=(n // tm,),
        in_specs=[row(D_MODEL), _const_spec(g.shape), _const_spec(wqk.shape), _const_spec(wv.shape),
                  _const_spec(wo.shape), _const_spec(wg.shape), _const_spec(wuv.shape),
                  _const_spec(vg.shape), _const_spec(ws.shape), _const_spec(bs_t.shape)],
        out_specs=[row(2 * A_WIDTH), row(A_WIDTH), row(A_WIDTH), row(4 * A_HEADS), row(B_WIDTH)],
        out_shape=[jax.ShapeDtypeStruct((n, 2 * A_WIDTH), F32),
                   jax.ShapeDtypeStruct((n, A_WIDTH), BF16),
                   jax.ShapeDtypeStruct((n, A_WIDTH), F32),
                   jax.ShapeDtypeStruct((n, 4 * A_HEADS), F32),
                   jax.ShapeDtypeStruct((n, B_WIDTH), BF16)],
        compiler_params=_params(1),
        name="ab_in",
    )(x2, g, wqk, wv, wo, wg, wuv, vg, ws, bs_t)


def _split3_dot(x, tri):
    x0 = x.astype(BF16)
    r1 = x - x0.astype(F32)
    x1 = r1.astype(BF16)
    x2 = (r1 - x1.astype(F32)).astype(BF16)
    return _dot(x0, tri) + _dot(x1, tri) + _dot(x2, tri)


def _mlstm_kernel(q_ref, k_ref, v_ref, oa_ref, gi_ref, gf_ref, gbi_ref, gbf_ref, cwq_ref, cwk_ref,
                  hg_ref, o_ref,
                  xp_ref, qs_ref, kt_ref, va_ref, a_ref, cum_ref, tot_ref, amx_ref, bcol_ref,
                  hf_ref, hb_ref, cf_ref, cb_ref):
    seq = q_ref.shape[1]
    nc = seq // CHUNK
    dh = A_HEAD_DIM
    halo = 8
    pad = (CONV_WIDTH - 1) // 2

    zeros_halo = jnp.zeros((halo, dh), F32)
    xp_ref[0:halo, :] = zeros_halo
    xp_ref[halo + seq:halo + seq + halo, :] = zeros_halo

    def conv_chunk(c, w):
        r0 = pl.multiple_of(c * CHUNK, CHUNK)
        win = xp_ref[pl.ds(r0, CHUNK + 2 * halo), :]
        y = win[halo - pad:halo - pad + CHUNK, :] * w[0:1, :]
        for j in range(1, CONV_WIDTH):
            y = y + win[halo - pad + j:halo - pad + j + CHUNK, :] * w[j:j + 1, :]
        return y * _sigmoid(y)

    xp_ref[halo:halo + seq, :] = q_ref[0]
    wq = cwq_ref[...]

    def q_body(c, carry):
        r0 = pl.multiple_of(c * CHUNK, CHUNK)
        qs_ref[pl.ds(r0, CHUNK), :] = (conv_chunk(c, wq) * (dh ** -0.5)).astype(BF16)
        return carry

    lax.fori_loop(0, nc, q_body, 0, unroll=2)

    xp_ref[halo:halo + seq, :] = k_ref[0]
    wk = cwk_ref[...]

    def k_body(c, carry):
        r0 = pl.multiple_of(c * CHUNK, CHUNK)
        kt_ref[:, pl.ds(r0, CHUNK)] = conv_chunk(c, wk).T.astype(BF16)
        return carry

    lax.fori_loop(0, nc, k_body, 0, unroll=2)

    va_ref[:, 0:dh] = v_ref[0]
    va_ref[:, dh:2 * dh] = jnp.ones((seq, dh), BF16)

    bias = lambda ref: jnp.concatenate([ref[0]] * nc, axis=0)
    li = gi_ref[0, 0] + bias(gbi_ref)
    fpre = gf_ref[0, 0] + bias(gbf_ref)
    lf = jnp.minimum(fpre, 0.0) - jnp.log1p(jnp.exp(-jnp.abs(fpre)))
    direction = lax.broadcasted_iota(jnp.int32, li.shape, 0) % 8
    ui = lax.broadcasted_iota(jnp.int32, (CHUNK, CHUNK), 0)
    si = lax.broadcasted_iota(jnp.int32, (CHUNK, CHUNK), 1)
    cum = (_split3_dot(jnp.where(direction == 0, lf, 0.0), jnp.where(ui <= si, 1.0, 0.0).astype(BF16))
           + _split3_dot(jnp.where(direction == 1, lf, 0.0), jnp.where(ui >= si, 1.0, 0.0).astype(BF16)))
    a_tab = li - cum
    total = jnp.where(direction == 0, cum[:, CHUNK - 1:CHUNK], cum[:, 0:1])
    a_ref[...] = a_tab
    tot_ref[...] = jnp.broadcast_to(total, a_tab.shape)
    amx_ref[...] = jnp.broadcast_to(jnp.max(a_tab, axis=-1, keepdims=True), a_tab.shape)

    cum_ref[...] = cum

    def bcol_body(c, carry):
        rs = pl.ds(pl.multiple_of(c * CHUNK, CHUNK), CHUNK)
        tile = cum_ref[pl.ds(pl.multiple_of(c * 8, 8), 8), :]
        for d in range(2):
            bcol_ref[d, rs, :] = jnp.broadcast_to(tile[d:d + 1, :], (CHUNK, CHUNK)).T
        return carry

    lax.fori_loop(0, nc, bcol_body, 0, unroll=2)

    cf_ref[...] = jnp.zeros_like(cf_ref)
    cb_ref[...] = jnp.zeros_like(cb_ref)

    def step(c, c_ref, m, d):
        r0 = pl.multiple_of(c * CHUNK, CHUNK)
        g0 = pl.multiple_of(c * 8, 8)
        qc = qs_ref[pl.ds(r0, CHUNK), :]
        ktc = kt_ref[:, pl.ds(r0, CHUNK)]
        vc = va_ref[pl.ds(r0, CHUNK), :]
        a = a_ref[pl.ds(g0, 8), :][d:d + 1, :]
        tot = tot_ref[pl.ds(g0, 8), :][d:d + 1, :]
        amax = amx_ref[pl.ds(g0, 8), :][d:d + 1, :]
        visible = (ui >= si) if d == 0 else (ui <= si)
        am = jnp.where(visible, a, -jnp.inf)
        reach = jnp.broadcast_to(jnp.max(am, axis=-1, keepdims=True), (CHUNK, CHUNK))
        m_row = jnp.maximum(m, reach)
        inter = jnp.exp(m - m_row)
        s = _dot(qc, ktc) * jnp.exp(am - m_row)
        m_new = tot + jnp.maximum(m, amax)
        decay = jnp.exp(tot + m - m_new)
        kw = (ktc.astype(F32) * jnp.exp(tot + a - m_new)).astype(BF16)
        sv = _dot(jnp.concatenate([s.astype(BF16), kw], axis=0), vc)
        nd = sv[0:CHUNK, :] + jnp.concatenate([inter, inter], axis=1) * _dot(qc, c_ref[...].astype(BF16))
        clamp = jnp.exp(-(bcol_ref[d, pl.ds(r0, CHUNK), :] + m_row))
        h = nd[:, 0:dh] / jnp.maximum(jnp.abs(nd[:, dh:2 * dh]), clamp)
        c_ref[...] = jnp.concatenate([decay, decay], axis=1) * c_ref[...] + sv[CHUNK:2 * CHUNK, :]
        return h, m_new

    def scan_body(c, carry):
        m_f, m_b = carry
        h_f, m_f = step(c, cf_ref, m_f, 0)
        hf_ref[pl.ds(pl.multiple_of(c * CHUNK, CHUNK), CHUNK), :] = h_f
        cb = nc - 1 - c
        h_b, m_b = step(cb, cb_ref, m_b, 1)
        hb_ref[pl.ds(pl.multiple_of(cb * CHUNK, CHUNK), CHUNK), :] = h_b
        return m_f, m_b

    m_init = jnp.zeros((1, CHUNK), F32)
    lax.fori_loop(0, nc, scan_body, (m_init, m_init), unroll=4)

    def out_body(c, carry):
        rs = pl.ds(pl.multiple_of(c * CHUNK, CHUNK), CHUNK)
        y = _rms(hf_ref[rs, :] + hb_ref[rs, :], hg_ref[...])
        o_ref[0, rs, :] = (y * _sigmoid(oa_ref[0, rs, :])).astype(BF16)
        return carry

    lax.fori_loop(0, nc, out_body, 0, unroll=2)


def _mlstm(qk, va, oa, gi, gf, gbi, gbf, conv_w, head_g):
    bsz, seq, _ = qk.shape
    nc = seq // CHUNK
    dh = A_HEAD_DIM
    col = lambda off: pl.BlockSpec((1, seq, dh), lambda b, h: (b, 0, h + off))
    gate_spec = pl.BlockSpec((1, 1, 8 * nc, CHUNK), lambda b, h: (b, h, 0, 0))
    bias_spec = pl.BlockSpec((1, 8, 1), lambda b, h: (h, 0, 0))
    table = pltpu.VMEM((8 * nc, CHUNK), F32)
    return pl.pallas_call(
        _mlstm_kernel,
        grid=(bsz, A_HEADS),
        in_specs=[col(0), col(A_HEADS), col(0), col(0), gate_spec, gate_spec, bias_spec, bias_spec,
                  pl.BlockSpec((CONV_WIDTH, dh), lambda b, h: (0, h)),
                  pl.BlockSpec((CONV_WIDTH, dh), lambda b, h: (0, h + A_HEADS)),
                  pl.BlockSpec((1, dh), lambda b, h: (0, h))],
        out_specs=col(0),
        out_shape=jax.ShapeDtypeStruct((bsz, seq, A_WIDTH), BF16),
        scratch_shapes=[pltpu.VMEM((seq + 16, dh), F32),
                        pltpu.VMEM((seq, dh), BF16),
                        pltpu.VMEM((dh, seq), BF16),
                        pltpu.VMEM((seq, 2 * dh), BF16),
                        table,
                        table,
                        table,
                        table,
                        pltpu.VMEM((2, seq, CHUNK), F32),
                        pltpu.VMEM((seq, dh), F32),
                        pltpu.VMEM((seq, dh), F32),
                        pltpu.VMEM((dh, 2 * dh), F32),
                        pltpu.VMEM((dh, 2 * dh), F32)],
        compiler_params=_params(2),
        name="mlstm",
    )(qk, qk, va, oa, gi, gf, gbi, gbf, conv_w, conv_w, head_g)


def _rope_table_kernel(pos_ref, freq_ref, cos_ref, sin_ref, nsin_ref):
    ang = pos_ref[...].astype(F32) * freq_ref[...]
    sn = jnp.sin(ang)
    cos_ref[...] = jnp.cos(ang)
    sin_ref[...] = sn
    nsin_ref[...] = -sn


def _rope_tables(pos_rep, freq_rep):
    rows = pos_rep.shape[0]
    tr = min(rows, 512)
    spec = pl.BlockSpec((tr, LANES), lambda i: (i, 0))
    out = jax.ShapeDtypeStruct((rows, LANES), F32)
    return pl.pallas_call(
        _rope_table_kernel,
        grid=(rows // tr,),
        in_specs=[spec, _const_spec(freq_rep.shape)],
        out_specs=[spec, spec, spec],
        out_shape=[out, out, out],
        compiler_params=_params(1),
        name="rope_tables",
    )(pos_rep, freq_rep)


def _mla_in_kernel(x_ref, g_ref, w1_ref, qg_ref, wab_ref, kvg_ref, wk_ref, wv_ref, ct_ref, st_ref,
                   q_ref, k_ref, v_ref):
    hp = HEAD_PAD
    h = _rms(x_ref[...], g_ref[...]).astype(BF16)
    p1 = _dot(h, w1_ref[...])
    ct = ct_ref[...]
    st = st_ref[...]
    k_rope = p1[:, Q_LORA + KV_LORA:Q_LORA + KV_LORA + hp] * ct + p1[:, Q_LORA + KV_LORA + hp:] * st

    cq = _rms(p1[:, :Q_LORA], qg_ref[...]).astype(BF16)
    ab = _dot(cq, wab_ref[...])
    scale = (C_NOPE + C_ROPE) ** -0.5 * LOG2E
    for hd in range(C_HEADS):
        a = ab[:, hd * hp:(hd + 1) * hp]
        b = ab[:, (C_HEADS + hd) * hp:(C_HEADS + hd + 1) * hp]
        q_ref[:, hd * hp:(hd + 1) * hp] = ((a * ct + b * st) * scale).astype(BF16)

    ckv = _rms(p1[:, Q_LORA:Q_LORA + KV_LORA], kvg_ref[...]).astype(BF16)
    kn = _dot(ckv, wk_ref[...])
    for hd in range(C_HEADS):
        k_ref[:, hd * hp:(hd + 1) * hp] = (kn[:, hd * hp:(hd + 1) * hp] + k_rope).astype(BF16)
    v_ref[...] = lax.dot_general(wv_ref[...], ckv, (((1,), (1,)), ((), ())),
                                 preferred_element_type=F32).astype(BF16)


def _mla_in(x2, g, w1, qg, wab, kvg, wk, wv, ct, st, tm=512):
    n = x2.shape[0]
    row = lambda w: pl.BlockSpec((tm, w), lambda i: (i, 0))
    return pl.pallas_call(
        _mla_in_kernel,
        grid=(n // tm,),
        in_specs=[row(D_MODEL), _const_spec(g.shape), _const_spec(w1.shape), _const_spec(qg.shape),
                  _const_spec(wab.shape), _const_spec(kvg.shape), _const_spec(wk.shape),
                  _const_spec(wv.shape), row(HEAD_PAD), row(HEAD_PAD)],
        out_specs=[row(C_HEADS * HEAD_PAD), row(C_HEADS * HEAD_PAD),
                   pl.BlockSpec((C_HEADS * C_V, tm), lambda i: (0, i))],
        out_shape=[jax.ShapeDtypeStruct((n, C_HEADS * HEAD_PAD), BF16),
                   jax.ShapeDtypeStruct((n, C_HEADS * HEAD_PAD), BF16),
                   jax.ShapeDtypeStruct((C_HEADS * C_V, n), BF16)],
        compiler_params=_params(1),
        name="mla_in",
    )(x2, g, w1, qg, wab, kvg, wk, wv, ct, st)


def _mla_attn_kernel(q_ref, k_ref, vt_ref, o_ref, vx_ref, s_ref, *, tq):
    seq = q_ref.shape[1]
    hp = HEAD_PAD
    n_tiles = seq // tq
    ones_rows = V_EXT_ROWS - C_V
    for hh in range(2):
        vx_ref[hh, 0:C_V, :] = vt_ref[hh * C_V:(hh + 1) * C_V, :]
        vx_ref[hh, C_V:V_EXT_ROWS, :] = jnp.ones((ones_rows, seq), BF16)

    def rows(t):
        return pl.ds(pl.multiple_of(t * tq, tq), tq)

    def scores(t):
        maxima = []
        for hh in range(2):
            q = q_ref[0, rows(t), hh * hp:(hh + 1) * hp]
            k = k_ref[0, :, hh * hp:(hh + 1) * hp]
            st = lax.dot_general(k, q, (((1,), (1,)), ((), ())), preferred_element_type=F32)
            s_ref[t % 2, hh] = st
            maxima.append(jnp.max(st, axis=0, keepdims=True))
        return tuple(maxima)

    def values(t, maxima):
        outs = []
        for hh in range(2):
            pt = jnp.exp2(s_ref[t % 2, hh] - maxima[hh]).astype(BF16)
            ot = _dot(vx_ref[hh], pt)
            outs.append(ot[0:C_V, :] / ot[C_V:C_V + 1, :])
        o_ref[0, rows(t), :] = jnp.concatenate(outs, axis=0).T.astype(BF16)

    def tile_body(t, m_prev):
        values(t - 1, m_prev)
        return scores(t)

    m_last = lax.fori_loop(1, n_tiles, tile_body, scores(0))
    values(n_tiles - 1, m_last)


def _mla_attn(q, k, vt, tq=256):
    bsz, seq, _ = q.shape
    qk_spec = pl.BlockSpec((1, seq, 2 * HEAD_PAD), lambda b, h: (b, 0, h))
    return pl.pallas_call(
        functools.partial(_mla_attn_kernel, tq=tq),
        grid=(bsz, C_HEADS // 2),
        in_specs=[qk_spec, qk_spec, pl.BlockSpec((2 * C_V, seq), lambda b, h: (h, b))],
        out_specs=pl.BlockSpec((1, seq, 2 * C_V), lambda b, h: (b, 0, h)),
        out_shape=jax.ShapeDtypeStruct((bsz, seq, C_HEADS * C_V), BF16),
        scratch_shapes=[pltpu.VMEM((2, V_EXT_ROWS, seq), BF16),
                        pltpu.VMEM((2, 2, seq, tq), F32)],
        compiler_params=_params(2),
        name="mla_attn",
    )(q, k, vt)


def _mix_ffn_kernel(*refs, n_mix, tf, final):
    x_ref = refs[0]
    mix_refs = refs[1:1 + n_mix]
    wmix_ref, g_ref, w1_ref, w2_ref = refs[1 + n_mix:5 + n_mix]
    fg_ref = refs[5 + n_mix] if final else None
    o_ref = refs[-1]

    mix = jnp.concatenate([a_ref[...] for a_ref in mix_refs], axis=1)
    x = x_ref[...] + _dot(mix, wmix_ref[...])
    h = _rms(x, g_ref[...]).astype(BF16)
    for j in range(D_FF // tf):
        a = _dot(h, w1_ref[:, j * tf:(j + 1) * tf])
        a = jnp.square(jnp.maximum(a, 0.0)).astype(BF16)
        x = x + _dot(a, w2_ref[j * tf:(j + 1) * tf, :])
    if final:
        x = _rms(x, fg_ref[...])
    o_ref[...] = x


def _mix_ffn(x2, mixes, wmix, g, w1, w2, final_g=None, tm=512, tf=512):
    n = x2.shape[0]
    row = lambda w: pl.BlockSpec((tm, w), lambda i: (i, 0))
    final = final_g is not None
    args = [x2, *mixes, wmix, g, w1, w2] + ([final_g] if final else [])
    in_specs = ([row(D_MODEL)] + [row(a.shape[1]) for a in mixes]
                + [_const_spec(a.shape) for a in args[1 + len(mixes):]])
    return pl.pallas_call(
        functools.partial(_mix_ffn_kernel, n_mix=len(mixes), tf=tf, final=final),
        grid=(n // tm,),
        in_specs=in_specs,
        out_specs=row(D_MODEL),
        out_shape=jax.ShapeDtypeStruct((n, D_MODEL), F32),
        compiler_params=_params(1),
        name="mix_ffn",
    )(*args)


def _pad_heads(w, width, parts):
    k = w.shape[0]
    w = w.reshape(k, C_HEADS, width)
    cols = [w[:, :, a:b] if a is not None else jnp.zeros((k, C_HEADS, b), w.dtype) for a, b in parts]
    used = sum(c.shape[2] for c in cols)
    cols.append(jnp.zeros((k, C_HEADS, HEAD_PAD - used), w.dtype))
    return jnp.concatenate(cols, axis=2).reshape(k, C_HEADS * HEAD_PAD)


def kernel(x, positions, ab_norm, ab_w_in, ab_conv, ab_gate_b, ab_head_g, ab_v_g, ab_ws, ab_bs,
           ab_w_out, c_norm, c_w_in, c_q_g, c_kv_g, c_w_uq, c_w_ukv, c_w_out, ffn_norm, ffn_w1,
           ffn_w2, final_norm):
    bsz, seq, _ = x.shape
    n = bsz * seq
    nc = seq // CHUNK
    half = C_ROPE // 2
    x2 = x.reshape(n, D_MODEL)

    freq = ROPE_BASE ** (-jnp.arange(half, dtype=F32) / half)
    pos_rep = jnp.repeat(positions.reshape(n), half).reshape(n * half // LANES, LANES)
    freq_rep = jnp.tile(freq, LANES // half).reshape(1, LANES)
    cos, sin, nsin = (t.reshape(n, half) for t in _rope_tables(pos_rep, freq_rep))
    ones = lambda w: jnp.ones((n, w), F32)
    zeros = lambda w: jnp.zeros((n, w), F32)
    cos_t = jnp.concatenate([ones(C_NOPE), cos, cos, ones(HEAD_PAD - C_NOPE - C_ROPE)], axis=1)
    sin_t = jnp.concatenate([zeros(C_NOPE), nsin, sin, zeros(HEAD_PAD - C_NOPE - C_ROPE)], axis=1)

    for layer in range(DEPTH):
        j = layer // 2
        final_g = final_norm.reshape(1, D_MODEL) if layer == DEPTH - 1 else None
        ffn_args = (ffn_norm[layer].reshape(1, D_MODEL), ffn_w1[layer].astype(BF16),
                    ffn_w2[layer].astype(BF16))
        if layer % 2 == 0:
            w = ab_w_in[j].astype(BF16)
            o0 = 4 * A_WIDTH
            o1 = o0 + 4 * A_HEADS
            qk, va, oa, gates, hb = _ab_in(
                x2, ab_norm[j].reshape(1, D_MODEL), w[:, :2 * A_WIDTH], w[:, 2 * A_WIDTH:3 * A_WIDTH],
                w[:, 3 * A_WIDTH:o0], w[:, o0:o1], w[:, o1:], ab_v_g[j].reshape(1, B_WIDTH),
                ab_ws[j].astype(BF16), ab_bs[j].T)
            gr = gates.reshape(bsz, nc, CHUNK, 2, 2, A_HEADS).transpose(4, 0, 5, 1, 3, 2)
            gr = jnp.pad(gr, ((0, 0),) * 4 + ((0, 6), (0, 0))).reshape(2, bsz, A_HEADS, 8 * nc, CHUNK)
            gb = jnp.pad(ab_gate_b[j].reshape(2, 2, A_HEADS).transpose(1, 2, 0), ((0, 0), (0, 0), (0, 6)))
            gb = gb.reshape(2, A_HEADS, 8, 1)
            ha = _mlstm(qk.reshape(bsz, seq, 2 * A_WIDTH), va.reshape(bsz, seq, A_WIDTH),
                        oa.reshape(bsz, seq, A_WIDTH), gr[0], gr[1], gb[0], gb[1], ab_conv[j],
                        ab_head_g[j].reshape(1, A_WIDTH))
            x2 = _mix_ffn(x2, [ha.reshape(n, A_WIDTH), hb], ab_w_out[j].astype(BF16),
                          *ffn_args, final_g=final_g)
        else:
            w_in = c_w_in[j]
            kr1 = w_in[:, Q_LORA + KV_LORA:Q_LORA + KV_LORA + half]
            kr2 = w_in[:, Q_LORA + KV_LORA + half:]
            zc = lambda w: jnp.zeros((D_MODEL, w), w_in.dtype)
            tail = HEAD_PAD - C_NOPE - C_ROPE
            w1 = jnp.concatenate([w_in[:, :Q_LORA + KV_LORA],
                                  zc(C_NOPE), kr1, kr2, zc(tail),
                                  zc(C_NOPE), kr2, kr1, zc(tail)], axis=1).astype(BF16)
            qw = C_NOPE + C_ROPE
            wa = _pad_heads(c_w_uq[j], qw, [(0, qw)])
            wb = _pad_heads(c_w_uq[j], qw, [(None, C_NOPE), (C_NOPE + half, qw), (C_NOPE, C_NOPE + half)])
            wab = jnp.concatenate([wa, wb], axis=1).astype(BF16)
            wk = _pad_heads(c_w_ukv[j], C_NOPE + C_V, [(0, C_NOPE)]).astype(BF16)
            wv = c_w_ukv[j].reshape(KV_LORA, C_HEADS, C_NOPE + C_V)[:, :, C_NOPE:]
            wv_t = wv.reshape(KV_LORA, C_HEADS * C_V).T.astype(BF16)
            q, k, vt = _mla_in(x2, c_norm[j].reshape(1, D_MODEL), w1, c_q_g[j].reshape(1, Q_LORA), wab,
                               c_kv_g[j].reshape(1, KV_LORA), wk, wv_t, cos_t, sin_t)
            o = _mla_attn(q.reshape(bsz, seq, -1), k.reshape(bsz, seq, -1), vt)
            x2 = _mix_ffn(x2, [o.reshape(n, C_HEADS * C_V)], c_w_out[j].astype(BF16),
                          *ffn_args, final_g=final_g)
    return x2.reshape(bsz, seq, D_MODEL)
```

```python
import functools

import jax
import jax.numpy as jnp
from jax import lax
from jax.experimental import pallas as pl
from jax.experimental.pallas import tpu as pltpu

F32 = jnp.float32
BF16 = jnp.bfloat16

D_MODEL = 1024
DEPTH = 4
A_WIDTH = 512
A_HEADS = 4
A_HEAD_DIM = 128
CHUNK = 128
CONV_WIDTH = 5
B_WIDTH = 512
B_GROUPS = 4
B_GROUP_DIM = 128
C_HEADS = 16
C_NOPE = 64
C_ROPE = 32
C_V = 64
Q_LORA = 384
KV_LORA = 256
ROPE_BASE = 10000.0
D_FF = 4 * D_MODEL
EPS = 1e-6

LANES = 128
HEAD_PAD = 128
V_EXT_ROWS = C_V + 16
LOG2E = 1.4426950408889634
KEY_BLOCK = 256
VMEM_LIMIT = 56 * 1024 * 1024


def _params(n_axes):
    return pltpu.CompilerParams(
        dimension_semantics=("arbitrary",) * n_axes, vmem_limit_bytes=VMEM_LIMIT)


def _const_spec(shape):
    nd = len(shape)
    return pl.BlockSpec(shape, lambda *_: (0,) * nd, pipeline_mode=pl.Buffered(1))


def _rms(x, g):
    return x * lax.rsqrt(jnp.mean(x * x, axis=-1, keepdims=True) + EPS) * g


def _dot(a, b):
    return jnp.dot(a, b, preferred_element_type=F32)


def _sigmoid(x):
    return 1.0 / (1.0 + jnp.exp(-x))


def _ab_in_kernel(x_ref, g_ref, wqk_ref, wv_ref, wo_ref, wg_ref, wuv_ref, vg_ref, ws_ref, bs_ref,
                  qk_ref, va_ref, oa_ref, gt_ref, hb_ref):
    tm = x_ref.shape[0]
    h = _rms(x_ref[...], g_ref[...]).astype(BF16)
    qk_ref[...] = _dot(h, wqk_ref[...])
    va_ref[...] = _dot(h, wv_ref[...]).astype(BF16)
    oa_ref[...] = _dot(h, wo_ref[...])
    gt_ref[...] = _dot(h, wg_ref[...])

    uv = _dot(h, wuv_ref[...])
    uv = 0.5 * uv * (1.0 + jnp.tanh(0.7978845608028654 * (uv + 0.044715 * (uv * uv * uv))))
    n_chunks = tm // CHUNK
    for g in range(B_GROUPS):
        lo = g * B_GROUP_DIM
        u = uv[:, lo:lo + B_GROUP_DIM]
        vb = uv[:, B_WIDTH + lo:B_WIDTH + lo + B_GROUP_DIM]
        vb = _rms(vb, vg_ref[:, lo:lo + B_GROUP_DIM]).astype(BF16)
        rhs = jnp.concatenate([vb[c * CHUNK:(c + 1) * CHUNK, :] for c in range(n_chunks)], axis=1)
        sp = _dot(ws_ref[g], rhs) + bs_ref[:, g:g + 1]
        for c in range(n_chunks):
            hb_ref[c * CHUNK:(c + 1) * CHUNK, lo:lo + B_GROUP_DIM] = (
                u[c * CHUNK:(c + 1) * CHUNK, :] * sp[:, c * B_GROUP_DIM:(c + 1) * B_GROUP_DIM]
            ).astype(BF16)


def _ab_in(x2, g, wqk, wv, wo, wg, wuv, vg, ws, bs_t, tm=512):
    n = x2.shape[0]
    row = lambda w: pl.BlockSpec((tm, w), lambda i: (i, 0))
    return pl.pallas_call(
        _ab_in_kernel,
        grid=(n // tm,),
        in_specs=[row(D_MODEL), _const_spec(g.shape), _const_spec(wqk.shape), _const_spec(wv.shape),
                  _const_spec(wo.shape), _const_spec(wg.shape), _const_spec(wuv.shape),
                  _const_spec(vg.shape), _const_spec(ws.shape), _const_spec(bs_t.shape)],
        out_specs=[row(2 * A_WIDTH), row(A_WIDTH), row(A_WIDTH), row(4 * A_HEADS), row(B_WIDTH)],
        out_shape=[jax.ShapeDtypeStruct((n, 2 * A_WIDTH), F32),
                   jax.ShapeDtypeStruct((n, A_WIDTH), BF16),
                   jax.ShapeDtypeStruct((n, A_WIDTH), F32),
                   jax.ShapeDtypeStruct((n, 4 * A_HEADS), F32),
                   jax.ShapeDtypeStruct((n, B_WIDTH), BF16)],
        compiler_params=_params(1),
        name="ab_in",
    )(x2, g, wqk, wv, wo, wg, wuv, vg, ws, bs_t)


def _split3_dot(x, tri):
    x0 = x.astype(BF16)
    r1 = x - x0.astype(F32)
    x1 = r1.astype(BF16)
    x2 = (r1 - x1.astype(F32)).astype(BF16)
    return _dot(x0, tri) + _dot(x1, tri) + _dot(x2, tri)


def _mlstm_kernel(q_ref, k_ref, v_ref, oa_ref, gi_ref, gf_ref, gbi_ref, gbf_ref, cwq_ref, cwk_ref,
                  hg_ref, o_ref,
                  xp_ref, qs_ref, kt_ref, va_ref, a_ref, cum_ref, tot_ref, amx_ref, bcol_ref,
                  hf_ref, hb_ref, cf_ref, cb_ref):
    seq = q_ref.shape[1]
    nc = seq // CHUNK
    dh = A_HEAD_DIM
    halo = 8
    pad = (CONV_WIDTH - 1) // 2

    zeros_halo = jnp.zeros((halo, dh), F32)
    xp_ref[0:halo, :] = zeros_halo
    xp_ref[halo + seq:halo + seq + halo, :] = zeros_halo

    def conv_chunk(c, w):
        r0 = pl.multiple_of(c * CHUNK, CHUNK)
        win = xp_ref[pl.ds(r0, CHUNK + 2 * halo), :]
        y = win[halo - pad:halo - pad + CHUNK, :] * w[0:1, :]
        for j in range(1, CONV_WIDTH):
            y = y + win[halo - pad + j:halo - pad + j + CHUNK, :] * w[j:j + 1, :]
        return y * _sigmoid(y)

    xp_ref[halo:halo + seq, :] = q_ref[0]
    wq = cwq_ref[...]

    def q_body(c, carry):
        r0 = pl.multiple_of(c * CHUNK, CHUNK)
        qs_ref[pl.ds(r0, CHUNK), :] = (conv_chunk(c, wq) * (dh ** -0.5)).astype(BF16)
        return carry

    lax.fori_loop(0, nc, q_body, 0, unroll=2)

    xp_ref[halo:halo + seq, :] = k_ref[0]
    wk = cwk_ref[...]

    def k_body(c, carry):
        r0 = pl.multiple_of(c * CHUNK, CHUNK)
        kt_ref[:, pl.ds(r0, CHUNK)] = conv_chunk(c, wk).T.astype(BF16)
        return carry

    lax.fori_loop(0, nc, k_body, 0, unroll=2)

    va_ref[:, 0:dh] = v_ref[0]
    va_ref[:, dh:2 * dh] = jnp.ones((seq, dh), BF16)

    bias = lambda ref: jnp.concatenate([ref[0]] * nc, axis=0)
    li = gi_ref[0, 0] + bias(gbi_ref)
    fpre = gf_ref[0, 0] + bias(gbf_ref)
    lf = jnp.minimum(fpre, 0.0) - jnp.log1p(jnp.exp(-jnp.abs(fpre)))
    direction = lax.broadcasted_iota(jnp.int32, li.shape, 0) % 8
    ui = lax.broadcasted_iota(jnp.int32, (CHUNK, CHUNK), 0)
    si = lax.broadcasted_iota(jnp.int32, (CHUNK, CHUNK), 1)
    cum = (_split3_dot(jnp.where(direction == 0, lf, 0.0), jnp.where(ui <= si, 1.0, 0.0).astype(BF16))
           + _split3_dot(jnp.where(direction == 1, lf, 0.0), jnp.where(ui >= si, 1.0, 0.0).astype(BF16)))
    a_tab = li - cum
    total = jnp.where(direction == 0, cum[:, CHUNK - 1:CHUNK], cum[:, 0:1])
    a_ref[...] = a_tab
    tot_ref[...] = jnp.broadcast_to(total, a_tab.shape)
    amx_ref[...] = jnp.broadcast_to(jnp.max(a_tab, axis=-1, keepdims=True), a_tab.shape)

    cum_ref[...] = cum

    def bcol_body(c, carry):
        rs = pl.ds(pl.multiple_of(c * CHUNK, CHUNK), CHUNK)
        tile = cum_ref[pl.ds(pl.multiple_of(c * 8, 8), 8), :]
        for d in range(2):
            bcol_ref[d, rs, :] = jnp.broadcast_to(tile[d:d + 1, :], (CHUNK, CHUNK)).T
        return carry

    lax.fori_loop(0, nc, bcol_body, 0, unroll=2)

    cf_ref[...] = jnp.zeros_like(cf_ref)
    cb_ref[...] = jnp.zeros_like(cb_ref)

    def step(c, c_ref, m, d):
        r0 = pl.multiple_of(c * CHUNK, CHUNK)
        g0 = pl.multiple_of(c * 8, 8)
        qc = qs_ref[pl.ds(r0, CHUNK), :]
        ktc = kt_ref[:, pl.ds(r0, CHUNK)]
        vc = va_ref[pl.ds(r0, CHUNK), :]
        a = a_ref[pl.ds(g0, 8), :][d:d + 1, :]
        tot = tot_ref[pl.ds(g0, 8), :][d:d + 1, :]
        amax = amx_ref[pl.ds(g0, 8), :][d:d + 1, :]
        visible = (ui >= si) if d == 0 else (ui <= si)
        am = jnp.where(visible, a, -jnp.inf)
        reach = jnp.broadcast_to(jnp.max(am, axis=-1, keepdims=True), (CHUNK, CHUNK))
        m_row = jnp.maximum(m, reach)
        inter = jnp.exp(m - m_row)
        s = _dot(qc, ktc) * jnp.exp(am - m_row)
        m_new = tot + jnp.maximum(m, amax)
        decay = jnp.exp(tot + m - m_new)
        kw = (ktc.astype(F32) * jnp.exp(tot + a - m_new)).astype(BF16)
        sv = _dot(jnp.concatenate([s.astype(BF16), kw], axis=0), vc)
        nd = sv[0:CHUNK, :] + jnp.concatenate([inter, inter], axis=1) * _dot(qc, c_ref[...].astype(BF16))
        clamp = jnp.exp(-(bcol_ref[d, pl.ds(r0, CHUNK), :] + m_row))
        h = nd[:, 0:dh] / jnp.maximum(jnp.abs(nd[:, dh:2 * dh]), clamp)
        c_ref[...] = jnp.concatenate([decay, decay], axis=1) * c_ref[...] + sv[CHUNK:2 * CHUNK, :]
        return h, m_new

    def scan_body(c, carry):
        m_f, m_b = carry
        h_f, m_f = step(c, cf_ref, m_f, 0)
        hf_ref[pl.ds(pl.multiple_of(c * CHUNK, CHUNK), CHUNK), :] = h_f
        cb = nc - 1 - c
        h_b, m_b = step(cb, cb_ref, m_b, 1)
        hb_ref[pl.ds(pl.multiple_of(cb * CHUNK, CHUNK), CHUNK), :] = h_b
        return m_f, m_b

    m_init = jnp.zeros((1, CHUNK), F32)
    lax.fori_loop(0, nc, scan_body, (m_init, m_init), unroll=4)

    def out_body(c, carry):
        rs = pl.ds(pl.multiple_of(c * CHUNK, CHUNK), CHUNK)
        y = _rms(hf_ref[rs, :] + hb_ref[rs, :], hg_ref[...])
        o_ref[0, rs, :] = (y * _sigmoid(oa_ref[0, rs, :])).astype(BF16)
        return carry

    lax.fori_loop(0, nc, out_body, 0, unroll=2)


def _mlstm(qk, va, oa, gi, gf, gbi, gbf, conv_w, head_g):
    bsz, seq, _ = qk.shape
    nc = seq // CHUNK
    dh = A_HEAD_DIM
    col = lambda off: pl.BlockSpec((1, seq, dh), lambda b, h: (b, 0, h + off))
    gate_spec = pl.BlockSpec((1, 1, 8 * nc, CHUNK), lambda b, h: (b, h, 0, 0))
    bias_spec = pl.BlockSpec((1, 8, 1), lambda b, h: (h, 0, 0))
    table = pltpu.VMEM((8 * nc, CHUNK), F32)
    return pl.pallas_call(
        _mlstm_kernel,
        grid=(bsz, A_HEADS),
        in_specs=[col(0), col(A_HEADS), col(0), col(0), gate_spec, gate_spec, bias_spec, bias_spec,
                  pl.BlockSpec((CONV_WIDTH, dh), lambda b, h: (0, h)),
                  pl.BlockSpec((CONV_WIDTH, dh), lambda b, h: (0, h + A_HEADS)),
                  pl.BlockSpec((1, dh), lambda b, h: (0, h))],
        out_specs=col(0),
        out_shape=jax.ShapeDtypeStruct((bsz, seq, A_WIDTH), BF16),
        scratch_shapes=[pltpu.VMEM((seq + 16, dh), F32),
                        pltpu.VMEM((seq, dh), BF16),
                        pltpu.VMEM((dh, seq), BF16),
                        pltpu.VMEM((seq, 2 * dh), BF16),
                        table,
                        table,
                        table,
                        table,
                        pltpu.VMEM((2, seq, CHUNK), F32),
                        pltpu.VMEM((seq, dh), F32),
                        pltpu.VMEM((seq, dh), F32),
                        pltpu.VMEM((dh, 2 * dh), F32),
                        pltpu.VMEM((dh, 2 * dh), F32)],
        compiler_params=_params(2),
        name="mlstm",
    )(qk, qk, va, oa, gi, gf, gbi, gbf, conv_w, conv_w, head_g)


def _rope_table_kernel(pos_ref, freq_ref, cos_ref, sin_ref, nsin_ref):
    ang = pos_ref[...].astype(F32) * freq_ref[...]
    sn = jnp.sin(ang)
    cos_ref[...] = jnp.cos(ang)
    sin_ref[...] = sn
    nsin_ref[...] = -sn


def _rope_tables(pos_rep, freq_rep):
    rows = pos_rep.shape[0]
    tr = min(rows, 512)
    spec = pl.BlockSpec((tr, LANES), lambda i: (i, 0))
    out = jax.ShapeDtypeStruct((rows, LANES), F32)
    return pl.pallas_call(
        _rope_table_kernel,
        grid=(rows // tr,),
        in_specs=[spec, _const_spec(freq_rep.shape)],
        out_specs=[spec, spec, spec],
        out_shape=[out, out, out],
        compiler_params=_params(1),
        name="rope_tables",
    )(pos_rep, freq_rep)


def _mla_in_kernel(x_ref, g_ref, w1_ref, qg_ref, wab_ref, kvg_ref, wk_ref, wv_ref, ct_ref, st_ref,
                   q_ref, k_ref, v_ref):
    hp = HEAD_PAD
    h = _rms(x_ref[...], g_ref[...]).astype(BF16)
    p1 = _dot(h, w1_ref[...])
    ct = ct_ref[...]
    st = st_ref[...]
    k_rope = p1[:, Q_LORA + KV_LORA:Q_LORA + KV_LORA + hp] * ct + p1[:, Q_LORA + KV_LORA + hp:] * st

    cq = _rms(p1[:, :Q_LORA], qg_ref[...]).astype(BF16)
    ab = _dot(cq, wab_ref[...])
    scale = (C_NOPE + C_ROPE) ** -0.5 * LOG2E
    for hd in range(C_HEADS):
        a = ab[:, hd * hp:(hd + 1) * hp]
        b = ab[:, (C_HEADS + hd) * hp:(C_HEADS + hd + 1) * hp]
        q_ref[:, hd * hp:(hd + 1) * hp] = ((a * ct + b * st) * scale).astype(BF16)

    ckv = _rms(p1[:, Q_LORA:Q_LORA + KV_LORA], kvg_ref[...]).astype(BF16)
    kn = _dot(ckv, wk_ref[...])
    for hd in range(C_HEADS):
        k_ref[:, hd * hp:(hd + 1) * hp] = (kn[:, hd * hp:(hd + 1) * hp] + k_rope).astype(BF16)
    v_ref[...] = lax.dot_general(wv_ref[...], ckv, (((1,), (1,)), ((), ())),
                                 preferred_element_type=F32).astype(BF16)


def _mla_in(x2, g, w1, qg, wab, kvg, wk, wv, ct, st, tm=512):
    n = x2.shape[0]
    row = lambda w: pl.BlockSpec((tm, w), lambda i: (i, 0))
    return pl.pallas_call(
        _mla_in_kernel,
        grid=(n // tm,),
        in_specs=[row(D_MODEL), _const_spec(g.shape), _const_spec(w1.shape), _const_spec(qg.shape),
                  _const_spec(wab.shape), _const_spec(kvg.shape), _const_spec(wk.shape),
                  _const_spec(wv.shape), row(HEAD_PAD), row(HEAD_PAD)],
        out_specs=[row(C_HEADS * HEAD_PAD), row(C_HEADS * HEAD_PAD),
                   pl.BlockSpec((C_HEADS * C_V, tm), lambda i: (0, i))],
        out_shape=[jax.ShapeDtypeStruct((n, C_HEADS * HEAD_PAD), BF16),
                   jax.ShapeDtypeStruct((n, C_HEADS * HEAD_PAD), BF16),
                   jax.ShapeDtypeStruct((C_HEADS * C_V, n), BF16)],
        compiler_params=_params(1),
        name="mla_in",
    )(x2, g, w1, qg, wab, kvg, wk, wv, ct, st)


def _mla_attn_kernel(q_ref, k_ref, vt_ref, o_ref, vx_ref, sa_ref, sb_ref, *, tq):
    seq = q_ref.shape[1]
    hp = HEAD_PAD
    n_tiles = seq // tq
    n_blocks = seq // KEY_BLOCK
    ones_rows = V_EXT_ROWS - C_V
    for hh in range(2):
        vx_ref[hh, 0:C_V, :] = vt_ref[hh * C_V:(hh + 1) * C_V, :]
        vx_ref[hh, C_V:V_EXT_ROWS, :] = jnp.ones((ones_rows, seq), BF16)

    def rows(t):
        return pl.ds(pl.multiple_of(t * tq, tq), tq)

    def keys(j):
        return slice(j * KEY_BLOCK, (j + 1) * KEY_BLOCK)

    def stage(score_tile, score_ref, value_tile, value_ref, m_prev):
        maxima = [None, None]
        accs = [None, None]
        if score_ref is not None:
            qts = [q_ref[0, rows(score_tile), hh * hp:(hh + 1) * hp].astype(F32).T.astype(BF16)
                   for hh in range(2)]
        for j in range(n_blocks):
            for hh in range(2):
                if score_ref is not None:
                    st = _dot(k_ref[0, keys(j), hh * hp:(hh + 1) * hp], qts[hh])
                    score_ref[hh, keys(j), :] = st
                    m = jnp.max(st, axis=0, keepdims=True)
                    maxima[hh] = m if j == 0 else jnp.maximum(maxima[hh], m)
                if value_ref is not None:
                    pt = jnp.exp2(value_ref[hh, keys(j), :] - m_prev[hh]).astype(BF16)
                    o = _dot(vx_ref[hh, :, keys(j)], pt)
                    accs[hh] = o if j == 0 else accs[hh] + o
        if value_ref is not None:
            outs = [acc[0:C_V, :] / acc[C_V:C_V + 1, :] for acc in accs]
            o_ref[0, rows(value_tile), :] = jnp.concatenate(outs, axis=0).T.astype(BF16)
        return tuple(maxima)

    def pair_body(i, m_prev):
        m_mid = stage(2 * i + 1, sb_ref, 2 * i, sa_ref, m_prev)
        return stage(2 * i + 2, sa_ref, 2 * i + 1, sb_ref, m_mid)

    m_even = lax.fori_loop(0, n_tiles // 2 - 1, pair_body, stage(0, sa_ref, None, None, None))
    m_odd = stage(n_tiles - 1, sb_ref, n_tiles - 2, sa_ref, m_even)
    stage(None, None, n_tiles - 1, sb_ref, m_odd)


def _mla_attn(q, k, vt, tq=256):
    bsz, seq, _ = q.shape
    qk_spec = pl.BlockSpec((1, seq, 2 * HEAD_PAD), lambda b, h: (b, 0, h))
    scores = pltpu.VMEM((2, seq, tq), F32)
    return pl.pallas_call(
        functools.partial(_mla_attn_kernel, tq=tq),
        grid=(bsz, C_HEADS // 2),
        in_specs=[qk_spec, qk_spec, pl.BlockSpec((2 * C_V, seq), lambda b, h: (h, b))],
        out_specs=pl.BlockSpec((1, seq, 2 * C_V), lambda b, h: (b, 0, h)),
        out_shape=jax.ShapeDtypeStruct((bsz, seq, C_HEADS * C_V), BF16),
        scratch_shapes=[pltpu.VMEM((2, V_EXT_ROWS, seq), BF16),
                        scores, scores],
        compiler_params=_params(2),
        name="mla_attn",
    )(q, k, vt)


def _mix_ffn_kernel(*refs, n_mix, tf, final):
    x_ref = refs[0]
    mix_refs = refs[1:1 + n_mix]
    wmix_ref, g_ref, w1_ref, w2_ref = refs[1 + n_mix:5 + n_mix]
    fg_ref = refs[5 + n_mix] if final else None
    o_ref = refs[-1]

    mix = jnp.concatenate([a_ref[...] for a_ref in mix_refs], axis=1)
    x = x_ref[...] + _dot(mix, wmix_ref[...])
    h = _rms(x, g_ref[...]).astype(BF16)
    for j in range(D_FF // tf):
        a = _dot(h, w1_ref[:, j * tf:(j + 1) * tf])
        a = jnp.square(jnp.maximum(a, 0.0)).astype(BF16)
        x = x + _dot(a, w2_ref[j * tf:(j + 1) * tf, :])
    if final:
        x = _rms(x, fg_ref[...])
    o_ref[...] = x


def _mix_ffn(x2, mixes, wmix, g, w1, w2, final_g=None, tm=512, tf=512):
    n = x2.shape[0]
    row = lambda w: pl.BlockSpec((tm, w), lambda i: (i, 0))
    final = final_g is not None
    args = [x2, *mixes, wmix, g, w1, w2] + ([final_g] if final else [])
    in_specs = ([row(D_MODEL)] + [row(a.shape[1]) for a in mixes]
                + [_const_spec(a.shape) for a in args[1 + len(mixes):]])
    return pl.pallas_call(
        functools.partial(_mix_ffn_kernel, n_mix=len(mixes), tf=tf, final=final),
        grid=(n // tm,),
        in_specs=in_specs,
        out_specs=row(D_MODEL),
        out_shape=jax.ShapeDtypeStruct((n, D_MODEL), F32),
        compiler_params=_params(1),
        name="mix_ffn",
    )(*args)


def _pad_heads(w, width, parts):
    k = w.shape[0]
    w = w.reshape(k, C_HEADS, width)
    cols = [w[:, :, a:b] if a is not None else jnp.zeros((k, C_HEADS, b), w.dtype) for a, b in parts]
    used = sum(c.shape[2] for c in cols)
    cols.append(jnp.zeros((k, C_HEADS, HEAD_PAD - used), w.dtype))
    return jnp.concatenate(cols, axis=2).reshape(k, C_HEADS * HEAD_PAD)


def kernel(x, positions, ab_norm, ab_w_in, ab_conv, ab_gate_b, ab_head_g, ab_v_g, ab_ws, ab_bs,
           ab_w_out, c_norm, c_w_in, c_q_g, c_kv_g, c_w_uq, c_w_ukv, c_w_out, ffn_norm, ffn_w1,
           ffn_w2, final_norm):
    bsz, seq, _ = x.shape
    n = bsz * seq
    nc = seq // CHUNK
    half = C_ROPE // 2
    x2 = x.reshape(n, D_MODEL)

    freq = ROPE_BASE ** (-jnp.arange(half, dtype=F32) / half)
    pos_rep = jnp.repeat(positions.reshape(n), half).reshape(n * half // LANES, LANES)
    freq_rep = jnp.tile(freq, LANES // half).reshape(1, LANES)
    cos, sin, nsin = (t.reshape(n, half) for t in _rope_tables(pos_rep, freq_rep))
    ones = lambda w: jnp.ones((n, w), F32)
    zeros = lambda w: jnp.zeros((n, w), F32)
    cos_t = jnp.concatenate([ones(C_NOPE), cos, cos, ones(HEAD_PAD - C_NOPE - C_ROPE)], axis=1)
    sin_t = jnp.concatenate([zeros(C_NOPE), nsin, sin, zeros(HEAD_PAD - C_NOPE - C_ROPE)], axis=1)

    for layer in range(DEPTH):
        j = layer // 2
        final_g = final_norm.reshape(1, D_MODEL) if layer == DEPTH - 1 else None
        ffn_args = (ffn_norm[layer].reshape(1, D_MODEL), ffn_w1[layer].astype(BF16),
                    ffn_w2[layer].astype(BF16))
        if layer % 2 == 0:
            w = ab_w_in[j].astype(BF16)
            o0 = 4 * A_WIDTH
            o1 = o0 + 4 * A_HEADS
            qk, va, oa, gates, hb = _ab_in(
                x2, ab_norm[j].reshape(1, D_MODEL), w[:, :2 * A_WIDTH], w[:, 2 * A_WIDTH:3 * A_WIDTH],
                w[:, 3 * A_WIDTH:o0], w[:, o0:o1], w[:, o1:], ab_v_g[j].reshape(1, B_WIDTH),
                ab_ws[j].astype(BF16), ab_bs[j].T)
            gr = gates.reshape(bsz, nc, CHUNK, 2, 2, A_HEADS).transpose(4, 0, 5, 1, 3, 2)
            gr = jnp.pad(gr, ((0, 0),) * 4 + ((0, 6), (0, 0))).reshape(2, bsz, A_HEADS, 8 * nc, CHUNK)
            gb = jnp.pad(ab_gate_b[j].reshape(2, 2, A_HEADS).transpose(1, 2, 0), ((0, 0), (0, 0), (0, 6)))
            gb = gb.reshape(2, A_HEADS, 8, 1)
            ha = _mlstm(qk.reshape(bsz, seq, 2 * A_WIDTH), va.reshape(bsz, seq, A_WIDTH),
                        oa.reshape(bsz, seq, A_WIDTH), gr[0], gr[1], gb[0], gb[1], ab_conv[j],
                        ab_head_g[j].reshape(1, A_WIDTH))
            x2 = _mix_ffn(x2, [ha.reshape(n, A_WIDTH), hb], ab_w_out[j].astype(BF16),
                          *ffn_args, final_g=final_g)
        else:
            w_in = c_w_in[j]
            kr1 = w_in[:, Q_LORA + KV_LORA:Q_LORA + KV_LORA + half]
            kr2 = w_in[:, Q_LORA + KV_LORA + half:]
            zc = lambda w: jnp.zeros((D_MODEL, w), w_in.dtype)
            tail = HEAD_PAD - C_NOPE - C_ROPE
            w1 = jnp.concatenate([w_in[:, :Q_LORA + KV_LORA],
                                  zc(C_NOPE), kr1, kr2, zc(tail),
                                  zc(C_NOPE), kr2, kr1, zc(tail)], axis=1).astype(BF16)
            qw = C_NOPE + C_ROPE
            wa = _pad_heads(c_w_uq[j], qw, [(0, qw)])
            wb = _pad_heads(c_w_uq[j], qw, [(None, C_NOPE), (C_NOPE + half, qw), (C_NOPE, C_NOPE + half)])
            wab = jnp.concatenate([wa, wb], axis=1).astype(BF16)
            wk = _pad_heads(c_w_ukv[j], C_NOPE + C_V, [(0, C_NOPE)]).astype(BF16)
            wv = c_w_ukv[j].reshape(KV_LORA, C_HEADS, C_NOPE + C_V)[:, :, C_NOPE:]
            wv_t = wv.reshape(KV_LORA, C_HEADS * C_V).T.astype(BF16)
            q, k, vt = _mla_in(x2, c_norm[j].reshape(1, D_MODEL), w1, c_q_g[j].reshape(1, Q_LORA), wab,
                               c_kv_g[j].reshape(1, KV_LORA), wk, wv_t, cos_t, sin_t)
            o = _mla_attn(q.reshape(bsz, seq, -1), k.reshape(bsz, seq, -1), vt)
            x2 = _mix_ffn(x2, [o.reshape(n, C_HEADS * C_V)], c_w_out[j].astype(BF16),
                          *ffn_args, final_g=final_g)
    return x2.reshape(bsz, seq, D_MODEL)
```

```python
import functools

import jax
import jax.numpy as jnp
from jax import lax
from jax.experimental import pallas as pl
from jax.experimental.pallas import tpu as pltpu

F32 = jnp.float32
BF16 = jnp.bfloat16

D_MODEL = 1024
DEPTH = 4
A_WIDTH = 512
A_HEADS = 4
A_HEAD_DIM = 128
CHUNK = 128
CONV_WIDTH = 5
B_WIDTH = 512
B_GROUPS = 4
B_GROUP_DIM = 128
C_HEADS = 16
C_NOPE = 64
C_ROPE = 32
C_V = 64
Q_LORA = 384
KV_LORA = 256
ROPE_BASE = 10000.0
D_FF = 4 * D_MODEL
EPS = 1e-6

LANES = 128
HEAD_PAD = 128
V_EXT_ROWS = C_V + 16
LOG2E = 1.4426950408889634
KEY_BLOCK = 256
ATTN_PAIRS = 2
VMEM_LIMIT = 56 * 1024 * 1024


def _params(n_axes):
    return pltpu.CompilerParams(
        dimension_semantics=("arbitrary",) * n_axes, vmem_limit_bytes=VMEM_LIMIT)


def _const_spec(shape):
    nd = len(shape)
    return pl.BlockSpec(shape, lambda *_: (0,) * nd, pipeline_mode=pl.Buffered(1))


def _rms(x, g):
    return x * lax.rsqrt(jnp.mean(x * x, axis=-1, keepdims=True) + EPS) * g


def _dot(a, b):
    return jnp.dot(a, b, preferred_element_type=F32)


def _sigmoid(x):
    return 1.0 / (1.0 + jnp.exp(-x))


def _ab_in_kernel(x_ref, g_ref, wqk_ref, wv_ref, wo_ref, wg_ref, wuv_ref, vg_ref, ws_ref, bs_ref,
                  qk_ref, va_ref, oa_ref, gt_ref, hb_ref):
    tm = x_ref.shape[0]
    h = _rms(x_ref[...], g_ref[...]).astype(BF16)
    qk_ref[...] = _dot(h, wqk_ref[...])
    va_ref[...] = _dot(h, wv_ref[...]).astype(BF16)
    oa_ref[...] = _dot(h, wo_ref[...])
    gt_ref[...] = _dot(h, wg_ref[...])

    uv = _dot(h, wuv_ref[...])
    uv = 0.5 * uv * (1.0 + jnp.tanh(0.7978845608028654 * (uv + 0.044715 * (uv * uv * uv))))
    n_chunks = tm // CHUNK
    for g in range(B_GROUPS):
        lo = g * B_GROUP_DIM
        u = uv[:, lo:lo + B_GROUP_DIM]
        vb = uv[:, B_WIDTH + lo:B_WIDTH + lo + B_GROUP_DIM]
        vb = _rms(vb, vg_ref[:, lo:lo + B_GROUP_DIM]).astype(BF16)
        rhs = jnp.concatenate([vb[c * CHUNK:(c + 1) * CHUNK, :] for c in range(n_chunks)], axis=1)
        sp = _dot(ws_ref[g], rhs) + bs_ref[:, g:g + 1]
        for c in range(n_chunks):
            hb_ref[c * CHUNK:(c + 1) * CHUNK, lo:lo + B_GROUP_DIM] = (
                u[c * CHUNK:(c + 1) * CHUNK, :] * sp[:, c * B_GROUP_DIM:(c + 1) * B_GROUP_DIM]
            ).astype(BF16)


def _ab_in(x2, g, wqk, wv, wo, wg, wuv, vg, ws, bs_t, tm=512):
    n = x2.shape[0]
    row = lambda w: pl.BlockSpec((tm, w), lambda i: (i, 0))
    return pl.pallas_call(
        _ab_in_kernel,
        grid=(n // tm,),
        in_specs=[row(D_MODEL), _const_spec(g.shape), _const_spec(wqk.shape), _const_spec(wv.shape),
                  _const_spec(wo.shape), _const_spec(wg.shape), _const_spec(wuv.shape),
                  _const_spec(vg.shape), _const_spec(ws.shape), _const_spec(bs_t.shape)],
        out_specs=[row(2 * A_WIDTH), row(A_WIDTH), row(A_WIDTH), row(4 * A_HEADS), row(B_WIDTH)],
        out_shape=[jax.ShapeDtypeStruct((n, 2 * A_WIDTH), F32),
                   jax.ShapeDtypeStruct((n, A_WIDTH), BF16),
                   jax.ShapeDtypeStruct((n, A_WIDTH), F32),
                   jax.ShapeDtypeStruct((n, 4 * A_HEADS), F32),
                   jax.ShapeDtypeStruct((n, B_WIDTH), BF16)],
        compiler_params=_params(1),
        name="ab_in",
    )(x2, g, wqk, wv, wo, wg, wuv, vg, ws, bs_t)


def _split3_dot(x, tri):
    x0 = x.astype(BF16)
    r1 = x - x0.astype(F32)
    x1 = r1.astype(BF16)
    x2 = (r1 - x1.astype(F32)).astype(BF16)
    return _dot(x0, tri) + _dot(x1, tri) + _dot(x2, tri)


def _mlstm_kernel(q_ref, k_ref, v_ref, oa_ref, gi_ref, gf_ref, gbi_ref, gbf_ref, cwq_ref, cwk_ref,
                  hg_ref, o_ref,
                  xp_ref, qs_ref, kt_ref, va_ref, a_ref, cum_ref, tot_ref, amx_ref, bcol_ref,
                  hf_ref, hb_ref, cf_ref, cb_ref):
    seq = q_ref.shape[1]
    nc = seq // CHUNK
    dh = A_HEAD_DIM
    halo = 8
    pad = (CONV_WIDTH - 1) // 2

    zeros_halo = jnp.zeros((halo, dh), F32)
    xp_ref[0:halo, :] = zeros_halo
    xp_ref[halo + seq:halo + seq + halo, :] = zeros_halo

    def conv_chunk(c, w):
        r0 = pl.multiple_of(c * CHUNK, CHUNK) + (halo - pad)
        y = xp_ref[pl.ds(r0, CHUNK), :] * w[0:1, :]
        for j in range(1, CONV_WIDTH):
            y = y + xp_ref[pl.ds(r0 + j, CHUNK), :] * w[j:j + 1, :]
        return y * _sigmoid(y)

    xp_ref[halo:halo + seq, :] = q_ref[0]
    wq = cwq_ref[...]

    def q_body(c, carry):
        r0 = pl.multiple_of(c * CHUNK, CHUNK)
        qs_ref[pl.ds(r0, CHUNK), :] = (conv_chunk(c, wq) * (dh ** -0.5)).astype(BF16)
        return carry

    lax.fori_loop(0, nc, q_body, 0, unroll=2)

    xp_ref[halo:halo + seq, :] = k_ref[0]
    wk = cwk_ref[...]

    def k_body(c, carry):
        r0 = pl.multiple_of(c * CHUNK, CHUNK)
        kt_ref[:, pl.ds(r0, CHUNK)] = conv_chunk(c, wk).T.astype(BF16)
        return carry

    lax.fori_loop(0, nc, k_body, 0, unroll=2)

    va_ref[:, 0:dh] = v_ref[0]
    va_ref[:, dh:2 * dh] = jnp.ones((seq, dh), BF16)

    bias = lambda ref: jnp.concatenate([ref[0]] * nc, axis=0)
    li = gi_ref[0, 0] + bias(gbi_ref)
    fpre = gf_ref[0, 0] + bias(gbf_ref)
    lf = jnp.minimum(fpre, 0.0) - jnp.log1p(jnp.exp(-jnp.abs(fpre)))
    direction = lax.broadcasted_iota(jnp.int32, li.shape, 0) % 8
    ui = lax.broadcasted_iota(jnp.int32, (CHUNK, CHUNK), 0)
    si = lax.broadcasted_iota(jnp.int32, (CHUNK, CHUNK), 1)
    cum = (_split3_dot(jnp.where(direction == 0, lf, 0.0), jnp.where(ui <= si, 1.0, 0.0).astype(BF16))
           + _split3_dot(jnp.where(direction == 1, lf, 0.0), jnp.where(ui >= si, 1.0, 0.0).astype(BF16)))
    a_tab = li - cum
    total = jnp.where(direction == 0, cum[:, CHUNK - 1:CHUNK], cum[:, 0:1])
    a_ref[...] = a_tab
    tot_ref[...] = jnp.broadcast_to(total, a_tab.shape)
    amx_ref[...] = jnp.broadcast_to(jnp.max(a_tab, axis=-1, keepdims=True), a_tab.shape)

    cum_ref[...] = cum

    def bcol_body(c, carry):
        rs = pl.ds(pl.multiple_of(c * CHUNK, CHUNK), CHUNK)
        tile = cum_ref[pl.ds(pl.multiple_of(c * 8, 8), 8), :]
        for d in range(2):
            bcol_ref[d, rs, :] = jnp.broadcast_to(tile[d:d + 1, :], (CHUNK, CHUNK)).T
        return carry

    lax.fori_loop(0, nc, bcol_body, 0, unroll=2)

    cf_ref[...] = jnp.zeros_like(cf_ref)
    cb_ref[...] = jnp.zeros_like(cb_ref)

    def step(c, c_ref, m, d):
        r0 = pl.multiple_of(c * CHUNK, CHUNK)
        g0 = pl.multiple_of(c * 8, 8)
        qc = qs_ref[pl.ds(r0, CHUNK), :]
        ktc = kt_ref[:, pl.ds(r0, CHUNK)]
        vc = va_ref[pl.ds(r0, CHUNK), :]
        a = a_ref[pl.ds(g0, 8), :][d:d + 1, :]
        tot = tot_ref[pl.ds(g0, 8), :][d:d + 1, :]
        amax = amx_ref[pl.ds(g0, 8), :][d:d + 1, :]
        visible = (ui >= si) if d == 0 else (ui <= si)
        am = jnp.where(visible, a, -jnp.inf)
        reach = jnp.broadcast_to(jnp.max(am, axis=-1, keepdims=True), (CHUNK, CHUNK))
        m_row = jnp.maximum(m, reach)
        inter = jnp.exp(m - m_row)
        s = _dot(qc, ktc) * jnp.exp(am - m_row)
        m_new = tot + jnp.maximum(m, amax)
        decay = jnp.exp(tot + m - m_new)
        kw = (ktc.astype(F32) * jnp.exp(tot + a - m_new)).astype(BF16)
        sv = _dot(jnp.concatenate([s.astype(BF16), kw], axis=0), vc)
        nd = sv[0:CHUNK, :] + jnp.concatenate([inter, inter], axis=1) * _dot(qc, c_ref[...].astype(BF16))
        clamp = jnp.exp(-(bcol_ref[d, pl.ds(r0, CHUNK), :] + m_row))
        h = nd[:, 0:dh] / jnp.maximum(jnp.abs(nd[:, dh:2 * dh]), clamp)
        c_ref[...] = jnp.concatenate([decay, decay], axis=1) * c_ref[...] + sv[CHUNK:2 * CHUNK, :]
        return h, m_new

    def scan_body(c, carry):
        m_f, m_b = carry
        h_f, m_f = step(c, cf_ref, m_f, 0)
        hf_ref[pl.ds(pl.multiple_of(c * CHUNK, CHUNK), CHUNK), :] = h_f
        cb = nc - 1 - c
        h_b, m_b = step(cb, cb_ref, m_b, 1)
        hb_ref[pl.ds(pl.multiple_of(cb * CHUNK, CHUNK), CHUNK), :] = h_b
        return m_f, m_b

    m_init = jnp.zeros((1, CHUNK), F32)
    lax.fori_loop(0, nc, scan_body, (m_init, m_init), unroll=4)

    def out_body(c, carry):
        rs = pl.ds(pl.multiple_of(c * CHUNK, CHUNK), CHUNK)
        y = _rms(hf_ref[rs, :] + hb_ref[rs, :], hg_ref[...])
        o_ref[0, rs, :] = (y * _sigmoid(oa_ref[0, rs, :])).astype(BF16)
        return carry

    lax.fori_loop(0, nc, out_body, 0, unroll=2)


def _mlstm(qk, va, oa, gi, gf, gbi, gbf, conv_w, head_g):
    bsz, seq, _ = qk.shape
    nc = seq // CHUNK
    dh = A_HEAD_DIM
    col = lambda off: pl.BlockSpec((1, seq, dh), lambda b, h: (b, 0, h + off))
    gate_spec = pl.BlockSpec((1, 1, 8 * nc, CHUNK), lambda b, h: (b, h, 0, 0))
    bias_spec = pl.BlockSpec((1, 8, 1), lambda b, h: (h, 0, 0))
    table = pltpu.VMEM((8 * nc, CHUNK), F32)
    return pl.pallas_call(
        _mlstm_kernel,
        grid=(bsz, A_HEADS),
        in_specs=[col(0), col(A_HEADS), col(0), col(0), gate_spec, gate_spec, bias_spec, bias_spec,
                  pl.BlockSpec((CONV_WIDTH, dh), lambda b, h: (0, h)),
                  pl.BlockSpec((CONV_WIDTH, dh), lambda b, h: (0, h + A_HEADS)),
                  pl.BlockSpec((1, dh), lambda b, h: (0, h))],
        out_specs=col(0),
        out_shape=jax.ShapeDtypeStruct((bsz, seq, A_WIDTH), BF16),
        scratch_shapes=[pltpu.VMEM((seq + 16, dh), F32),
                        pltpu.VMEM((seq, dh), BF16),
                        pltpu.VMEM((dh, seq), BF16),
                        pltpu.VMEM((seq, 2 * dh), BF16),
                        table,
                        table,
                        table,
                        table,
                        pltpu.VMEM((2, seq, CHUNK), F32),
                        pltpu.VMEM((seq, dh), F32),
                        pltpu.VMEM((seq, dh), F32),
                        pltpu.VMEM((dh, 2 * dh), F32),
                        pltpu.VMEM((dh, 2 * dh), F32)],
        compiler_params=_params(2),
        name="mlstm",
    )(qk, qk, va, oa, gi, gf, gbi, gbf, conv_w, conv_w, head_g)


def _rope_table_kernel(pos_ref, freq_ref, cos_ref, sin_ref, nsin_ref):
    ang = pos_ref[...].astype(F32) * freq_ref[...]
    sn = jnp.sin(ang)
    cos_ref[...] = jnp.cos(ang)
    sin_ref[...] = sn
    nsin_ref[...] = -sn


def _rope_tables(pos_rep, freq_rep):
    rows = pos_rep.shape[0]
    tr = min(rows, 512)
    spec = pl.BlockSpec((tr, LANES), lambda i: (i, 0))
    out = jax.ShapeDtypeStruct((rows, LANES), F32)
    return pl.pallas_call(
        _rope_table_kernel,
        grid=(rows // tr,),
        in_specs=[spec, _const_spec(freq_rep.shape)],
        out_specs=[spec, spec, spec],
        out_shape=[out, out, out],
        compiler_params=_params(1),
        name="rope_tables",
    )(pos_rep, freq_rep)


def _mla_in_kernel(x_ref, g_ref, w1_ref, qg_ref, wq_ref, kvg_ref, wk_ref, wv_ref, ct_ref, st_ref,
                   q_ref, k_ref, v_ref):
    hp = HEAD_PAD
    h = _rms(x_ref[...], g_ref[...]).astype(BF16)
    p1 = _dot(h, w1_ref[...])
    ct = ct_ref[...]
    st = st_ref[...]

    def rotary(a):
        return a * ct + pltpu.roll(a, hp // 2, 1) * st

    k_rope = rotary(p1[:, Q_LORA + KV_LORA:])

    cq = _rms(p1[:, :Q_LORA], qg_ref[...]).astype(BF16)
    qa = _dot(cq, wq_ref[...])
    scale = (C_NOPE + C_ROPE) ** -0.5 * LOG2E
    for hd in range(C_HEADS):
        q_ref[:, hd * hp:(hd + 1) * hp] = (rotary(qa[:, hd * hp:(hd + 1) * hp]) * scale).astype(BF16)

    ckv = _rms(p1[:, Q_LORA:Q_LORA + KV_LORA], kvg_ref[...]).astype(BF16)
    kn = _dot(ckv, wk_ref[...])
    for hd in range(C_HEADS):
        k_ref[:, hd * hp:(hd + 1) * hp] = (kn[:, hd * hp:(hd + 1) * hp] + k_rope).astype(BF16)
    v_ref[...] = lax.dot_general(wv_ref[...], ckv, (((1,), (1,)), ((), ())),
                                 preferred_element_type=F32).astype(BF16)


def _mla_in(x2, g, w1, qg, wab, kvg, wk, wv, ct, st, tm=512):
    n = x2.shape[0]
    row = lambda w: pl.BlockSpec((tm, w), lambda i: (i, 0))
    return pl.pallas_call(
        _mla_in_kernel,
        grid=(n // tm,),
        in_specs=[row(D_MODEL), _const_spec(g.shape), _const_spec(w1.shape), _const_spec(qg.shape),
                  _const_spec(wab.shape), _const_spec(kvg.shape), _const_spec(wk.shape),
                  _const_spec(wv.shape), row(HEAD_PAD), row(HEAD_PAD)],
        out_specs=[row(C_HEADS * HEAD_PAD), row(C_HEADS * HEAD_PAD),
                   pl.BlockSpec((C_HEADS * C_V, tm), lambda i: (0, i))],
        out_shape=[jax.ShapeDtypeStruct((n, C_HEADS * HEAD_PAD), BF16),
                   jax.ShapeDtypeStruct((n, C_HEADS * HEAD_PAD), BF16),
                   jax.ShapeDtypeStruct((C_HEADS * C_V, n), BF16)],
        compiler_params=_params(1),
        name="mla_in",
    )(x2, g, w1, qg, wab, kvg, wk, wv, ct, st)


def _mla_attn_kernel(q_ref, k_ref, vt_ref, o_ref, vx_ref, sa_ref, sb_ref, *, tq):
    seq = q_ref.shape[1]
    hp = HEAD_PAD
    n_tiles = seq // tq
    n_blocks = seq // KEY_BLOCK
    n_items = ATTN_PAIRS * n_tiles
    ones_rows = V_EXT_ROWS - C_V
    for hd in range(2 * ATTN_PAIRS):
        vx_ref[hd, 0:C_V, :] = vt_ref[hd * C_V:(hd + 1) * C_V, :]
        vx_ref[hd, C_V:V_EXT_ROWS, :] = jnp.ones((ones_rows, seq), BF16)

    def keys(j):
        return slice(j * KEY_BLOCK, (j + 1) * KEY_BLOCK)

    def stage(score_item, score_ref, value_item, value_ref, m_prev):
        maxima = [None, None]
        accs = [None, None]
        if score_ref is not None:
            sp = score_item // n_tiles
            srows = pl.ds(pl.multiple_of((score_item % n_tiles) * tq, tq), tq)
            slanes = [pl.ds(pl.multiple_of((2 * sp + hh) * hp, hp), hp) for hh in range(2)]
            qts = [q_ref[0, srows, slanes[hh]].astype(F32).T.astype(BF16) for hh in range(2)]
        if value_ref is not None:
            vp = value_item // n_tiles
            vrows = pl.ds(pl.multiple_of((value_item % n_tiles) * tq, tq), tq)
        for j in range(n_blocks):
            for hh in range(2):
                if score_ref is not None:
                    st = _dot(k_ref[0, keys(j), slanes[hh]], qts[hh])
                    score_ref[hh, keys(j), :] = st
                    m = jnp.max(st, axis=0, keepdims=True)
                    maxima[hh] = m if j == 0 else jnp.maximum(maxima[hh], m)
                if value_ref is not None:
                    pt = jnp.exp2(value_ref[hh, keys(j), :] - m_prev[hh]).astype(BF16)
                    o = _dot(vx_ref[2 * vp + hh, :, keys(j)], pt)
                    accs[hh] = o if j == 0 else accs[hh] + o
        if value_ref is not None:
            outs = [acc[0:C_V, :] / acc[C_V:C_V + 1, :] for acc in accs]
            olanes = pl.ds(pl.multiple_of(vp * 2 * C_V, 2 * C_V), 2 * C_V)
            o_ref[0, vrows, olanes] = jnp.concatenate(outs, axis=0).T.astype(BF16)
        return tuple(maxima)

    def pair_body(i, m_prev):
        m_mid = stage(2 * i + 1, sb_ref, 2 * i, sa_ref, m_prev)
        return stage(2 * i + 2, sa_ref, 2 * i + 1, sb_ref, m_mid)

    m_even = lax.fori_loop(0, n_items // 2 - 1, pair_body, stage(0, sa_ref, None, None, None))
    m_odd = stage(n_items - 1, sb_ref, n_items - 2, sa_ref, m_even)
    stage(None, None, n_items - 1, sb_ref, m_odd)


def _mla_attn(q, k, vt, tq=256):
    bsz, seq, _ = q.shape
    heads = 2 * ATTN_PAIRS
    qk_spec = pl.BlockSpec((1, seq, heads * HEAD_PAD), lambda b, h: (b, 0, h))
    scores = pltpu.VMEM((2, seq, tq), F32)
    return pl.pallas_call(
        functools.partial(_mla_attn_kernel, tq=tq),
        grid=(bsz, C_HEADS // heads),
        in_specs=[qk_spec, qk_spec, pl.BlockSpec((heads * C_V, seq), lambda b, h: (h, b))],
        out_specs=pl.BlockSpec((1, seq, heads * C_V), lambda b, h: (b, 0, h)),
        out_shape=jax.ShapeDtypeStruct((bsz, seq, C_HEADS * C_V), BF16),
        scratch_shapes=[pltpu.VMEM((heads, V_EXT_ROWS, seq), BF16),
                        scores, scores],
        compiler_params=_params(2),
        name="mla_attn",
    )(q, k, vt)


def _mix_ffn_kernel(*refs, n_mix, tf, final):
    x_ref = refs[0]
    mix_refs = refs[1:1 + n_mix]
    wmix_ref, g_ref, w1_ref, w2_ref = refs[1 + n_mix:5 + n_mix]
    fg_ref = refs[5 + n_mix] if final else None
    o_ref = refs[-1]

    mix = jnp.concatenate([a_ref[...] for a_ref in mix_refs], axis=1)
    x = x_ref[...] + _dot(mix, wmix_ref[...])
    h = _rms(x, g_ref[...]).astype(BF16)
    for j in range(D_FF // tf):
        a = _dot(h, w1_ref[:, j * tf:(j + 1) * tf])
        a = jnp.square(jnp.maximum(a, 0.0)).astype(BF16)
        x = x + _dot(a, w2_ref[j * tf:(j + 1) * tf, :])
    if final:
        x = _rms(x, fg_ref[...])
    o_ref[...] = x


def _mix_ffn(x2, mixes, wmix, g, w1, w2, final_g=None, tm=512, tf=512):
    n = x2.shape[0]
    row = lambda w: pl.BlockSpec((tm, w), lambda i: (i, 0))
    final = final_g is not None
    args = [x2, *mixes, wmix, g, w1, w2] + ([final_g] if final else [])
    in_specs = ([row(D_MODEL)] + [row(a.shape[1]) for a in mixes]
                + [_const_spec(a.shape) for a in args[1 + len(mixes):]])
    return pl.pallas_call(
        functools.partial(_mix_ffn_kernel, n_mix=len(mixes), tf=tf, final=final),
        grid=(n // tm,),
        in_specs=in_specs,
        out_specs=row(D_MODEL),
        out_shape=jax.ShapeDtypeStruct((n, D_MODEL), F32),
        compiler_params=_params(1),
        name="mix_ffn",
    )(*args)


def _pad_heads(w, width, parts):
    k = w.shape[0]
    w = w.reshape(k, C_HEADS, width)
    cols = [w[:, :, a:b] if a is not None else jnp.zeros((k, C_HEADS, b), w.dtype) for a, b in parts]
    used = sum(c.shape[2] for c in cols)
    cols.append(jnp.zeros((k, C_HEADS, HEAD_PAD - used), w.dtype))
    return jnp.concatenate(cols, axis=2).reshape(k, C_HEADS * HEAD_PAD)


def kernel(x, positions, ab_norm, ab_w_in, ab_conv, ab_gate_b, ab_head_g, ab_v_g, ab_ws, ab_bs,
           ab_w_out, c_norm, c_w_in, c_q_g, c_kv_g, c_w_uq, c_w_ukv, c_w_out, ffn_norm, ffn_w1,
           ffn_w2, final_norm):
    bsz, seq, _ = x.shape
    n = bsz * seq
    nc = seq // CHUNK
    half = C_ROPE // 2
    x2 = x.reshape(n, D_MODEL)

    freq = ROPE_BASE ** (-jnp.arange(half, dtype=F32) / half)
    pos_rep = jnp.repeat(positions.reshape(n), half).reshape(n * half // LANES, LANES)
    freq_rep = jnp.tile(freq, LANES // half).reshape(1, LANES)
    cos, sin, nsin = (t.reshape(n, half) for t in _rope_tables(pos_rep, freq_rep))
    ones = lambda w: jnp.ones((n, w), F32)
    zeros = lambda w: jnp.zeros((n, w), F32)
    gap = HEAD_PAD // 2 - half
    cos_t = jnp.concatenate([cos, ones(gap), cos, ones(gap)], axis=1)
    sin_t = jnp.concatenate([nsin, zeros(gap), sin, zeros(gap)], axis=1)

    for layer in range(DEPTH):
        j = layer // 2
        final_g = final_norm.reshape(1, D_MODEL) if layer == DEPTH - 1 else None
        ffn_args = (ffn_norm[layer].reshape(1, D_MODEL), ffn_w1[layer].astype(BF16),
                    ffn_w2[layer].astype(BF16))
        if layer % 2 == 0:
            w = ab_w_in[j].astype(BF16)
            o0 = 4 * A_WIDTH
            o1 = o0 + 4 * A_HEADS
            qk, va, oa, gates, hb = _ab_in(
                x2, ab_norm[j].reshape(1, D_MODEL), w[:, :2 * A_WIDTH], w[:, 2 * A_WIDTH:3 * A_WIDTH],
                w[:, 3 * A_WIDTH:o0], w[:, o0:o1], w[:, o1:], ab_v_g[j].reshape(1, B_WIDTH),
                ab_ws[j].astype(BF16), ab_bs[j].T)
            gr = gates.reshape(bsz, nc, CHUNK, 2, 2, A_HEADS).transpose(4, 0, 5, 1, 3, 2)
            gr = jnp.pad(gr, ((0, 0),) * 4 + ((0, 6), (0, 0))).reshape(2, bsz, A_HEADS, 8 * nc, CHUNK)
            gb = jnp.pad(ab_gate_b[j].reshape(2, 2, A_HEADS).transpose(1, 2, 0), ((0, 0), (0, 0), (0, 6)))
            gb = gb.reshape(2, A_HEADS, 8, 1)
            ha = _mlstm(qk.reshape(bsz, seq, 2 * A_WIDTH), va.reshape(bsz, seq, A_WIDTH),
                        oa.reshape(bsz, seq, A_WIDTH), gr[0], gr[1], gb[0], gb[1], ab_conv[j],
                        ab_head_g[j].reshape(1, A_WIDTH))
            x2 = _mix_ffn(x2, [ha.reshape(n, A_WIDTH), hb], ab_w_out[j].astype(BF16),
                          *ffn_args, final_g=final_g)
        else:
            w_in = c_w_in[j]
            kr1 = w_in[:, Q_LORA + KV_LORA:Q_LORA + KV_LORA + half]
            kr2 = w_in[:, Q_LORA + KV_LORA + half:]
            zc = lambda w: jnp.zeros((D_MODEL, w), w_in.dtype)
            w1 = jnp.concatenate([w_in[:, :Q_LORA + KV_LORA], kr1, zc(gap), kr2, zc(gap)],
                                 axis=1).astype(BF16)
            qw = C_NOPE + C_ROPE
            wab = _pad_heads(c_w_uq[j], qw, [(C_NOPE, C_NOPE + half), (0, gap), (C_NOPE + half, qw),
                                             (gap, C_NOPE)]).astype(BF16)
            wk = _pad_heads(c_w_ukv[j], C_NOPE + C_V, [(None, half), (0, gap), (None, half),
                                                       (gap, C_NOPE)]).astype(BF16)
            wv = c_w_ukv[j].reshape(KV_LORA, C_HEADS, C_NOPE + C_V)[:, :, C_NOPE:]
            wv_t = wv.reshape(KV_LORA, C_HEADS * C_V).T.astype(BF16)
            q, k, vt = _mla_in(x2, c_norm[j].reshape(1, D_MODEL), w1, c_q_g[j].reshape(1, Q_LORA), wab,
                               c_kv_g[j].reshape(1, KV_LORA), wk, wv_t, cos_t, sin_t)
            o = _mla_attn(q.reshape(bsz, seq, -1), k.reshape(bsz, seq, -1), vt)
            x2 = _mix_ffn(x2, [o.reshape(n, C_HEADS * C_V)], c_w_out[j].astype(BF16),
                          *ffn_args, final_g=final_g)
    return x2.reshape(bsz, seq, D_MODEL)
```

```python
import functools

import jax
import jax.numpy as jnp
from jax import lax
from jax.experimental import pallas as pl
from jax.experimental.pallas import tpu as pltpu

F32 = jnp.float32
BF16 = jnp.bfloat16

D_MODEL = 1024
DEPTH = 4
A_WIDTH = 512
A_HEADS = 4
A_HEAD_DIM = 128
CHUNK = 128
CONV_WIDTH = 5
B_WIDTH = 512
B_GROUPS = 4
B_GROUP_DIM = 128
C_HEADS = 16
C_NOPE = 64
C_ROPE = 32
C_V = 64
Q_LORA = 384
KV_LORA = 256
ROPE_BASE = 10000.0
D_FF = 4 * D_MODEL
EPS = 1e-6

LANES = 128
HEAD_PAD = 128
V_EXT_ROWS = C_V + 16
LOG2E = 1.4426950408889634
KEY_BLOCK = 256
ATTN_PAIRS = 2
VMEM_LIMIT = 56 * 1024 * 1024


def _params(n_axes):
    return pltpu.CompilerParams(
        dimension_semantics=("arbitrary",) * n_axes, vmem_limit_bytes=VMEM_LIMIT)


def _const_spec(shape):
    nd = len(shape)
    return pl.BlockSpec(shape, lambda *_: (0,) * nd, pipeline_mode=pl.Buffered(1))


def _rms(x, g):
    return x * lax.rsqrt(jnp.mean(x * x, axis=-1, keepdims=True) + EPS) * g


def _dot(a, b):
    return jnp.dot(a, b, preferred_element_type=F32)


def _sigmoid(x):
    return 1.0 / (1.0 + jnp.exp(-x))


def _ab_in_kernel(x_ref, g_ref, wqk_ref, wv_ref, wo_ref, wg_ref, wuv_ref, vg_ref, ws_ref, bs_ref,
                  qk_ref, va_ref, oa_ref, gt_ref, hb_ref):
    tm = x_ref.shape[0]
    h = _rms(x_ref[...], g_ref[...]).astype(BF16)
    qk_ref[...] = _dot(h, wqk_ref[...])
    va_ref[...] = _dot(h, wv_ref[...]).astype(BF16)
    oa_ref[...] = _dot(h, wo_ref[...])
    gt_ref[...] = _dot(h, wg_ref[...])

    uv = _dot(h, wuv_ref[...])
    uv = 0.5 * uv * (1.0 + jnp.tanh(0.7978845608028654 * (uv + 0.044715 * (uv * uv * uv))))
    n_chunks = tm // CHUNK
    for g in range(B_GROUPS):
        lo = g * B_GROUP_DIM
        u = uv[:, lo:lo + B_GROUP_DIM]
        vb = uv[:, B_WIDTH + lo:B_WIDTH + lo + B_GROUP_DIM]
        vb = _rms(vb, vg_ref[:, lo:lo + B_GROUP_DIM]).astype(BF16)
        rhs = jnp.concatenate([vb[c * CHUNK:(c + 1) * CHUNK, :] for c in range(n_chunks)], axis=1)
        sp = _dot(ws_ref[g], rhs) + bs_ref[:, g:g + 1]
        for c in range(n_chunks):
            hb_ref[c * CHUNK:(c + 1) * CHUNK, lo:lo + B_GROUP_DIM] = (
                u[c * CHUNK:(c + 1) * CHUNK, :] * sp[:, c * B_GROUP_DIM:(c + 1) * B_GROUP_DIM]
            ).astype(BF16)


def _ab_in(x2, g, wqk, wv, wo, wg, wuv, vg, ws, bs_t, tm=1024):
    n = x2.shape[0]
    row = lambda w: pl.BlockSpec((tm, w), lambda i: (i, 0))
    return pl.pallas_call(
        _ab_in_kernel,
        grid=(n // tm,),
        in_specs=[row(D_MODEL), _const_spec(g.shape), _const_spec(wqk.shape), _const_spec(wv.shape),
                  _const_spec(wo.shape), _const_spec(wg.shape), _const_spec(wuv.shape),
                  _const_spec(vg.shape), _const_spec(ws.shape), _const_spec(bs_t.shape)],
        out_specs=[row(2 * A_WIDTH), row(A_WIDTH), row(A_WIDTH), row(4 * A_HEADS), row(B_WIDTH)],
        out_shape=[jax.ShapeDtypeStruct((n, 2 * A_WIDTH), F32),
                   jax.ShapeDtypeStruct((n, A_WIDTH), BF16),
                   jax.ShapeDtypeStruct((n, A_WIDTH), F32),
                   jax.ShapeDtypeStruct((n, 4 * A_HEADS), F32),
                   jax.ShapeDtypeStruct((n, B_WIDTH), BF16)],
        compiler_params=_params(1),
        name="ab_in",
    )(x2, g, wqk, wv, wo, wg, wuv, vg, ws, bs_t)


def _split3_dot(x, tri):
    x0 = x.astype(BF16)
    r1 = x - x0.astype(F32)
    x1 = r1.astype(BF16)
    x2 = (r1 - x1.astype(F32)).astype(BF16)
    return _dot(x0, tri) + _dot(x1, tri) + _dot(x2, tri)


def _mlstm_kernel(q_ref, k_ref, v_ref, oa_ref, gi_ref, gf_ref, gbi_ref, gbf_ref, cwq_ref, cwk_ref,
                  hg_ref, o_ref,
                  xp_ref, qs_ref, kt_ref, va_ref, a_ref, cum_ref, tot_ref, amx_ref, bcol_ref,
                  hf_ref, hb_ref, cf_ref, cb_ref):
    seq = q_ref.shape[1]
    nc = seq // CHUNK
    dh = A_HEAD_DIM
    halo = 8
    pad = (CONV_WIDTH - 1) // 2

    zeros_halo = jnp.zeros((halo, dh), F32)
    xp_ref[0:halo, :] = zeros_halo
    xp_ref[halo + seq:halo + seq + halo, :] = zeros_halo

    def conv_chunk(c, w):
        r0 = pl.multiple_of(c * CHUNK, CHUNK) + (halo - pad)
        y = xp_ref[pl.ds(r0, CHUNK), :] * w[0:1, :]
        for j in range(1, CONV_WIDTH):
            y = y + xp_ref[pl.ds(r0 + j, CHUNK), :] * w[j:j + 1, :]
        return y * _sigmoid(y)

    xp_ref[halo:halo + seq, :] = q_ref[0]
    wq = cwq_ref[...]

    def q_body(c, carry):
        r0 = pl.multiple_of(c * CHUNK, CHUNK)
        qs_ref[pl.ds(r0, CHUNK), :] = (conv_chunk(c, wq) * (dh ** -0.5)).astype(BF16)
        return carry

    lax.fori_loop(0, nc, q_body, 0, unroll=2)

    xp_ref[halo:halo + seq, :] = k_ref[0]
    wk = cwk_ref[...]

    def k_body(c, carry):
        r0 = pl.multiple_of(c * CHUNK, CHUNK)
        kt_ref[:, pl.ds(r0, CHUNK)] = conv_chunk(c, wk).T.astype(BF16)
        return carry

    lax.fori_loop(0, nc, k_body, 0, unroll=2)

    va_ref[:, 0:dh] = v_ref[0]
    va_ref[:, dh:2 * dh] = jnp.ones((seq, dh), BF16)

    bias = lambda ref: jnp.concatenate([ref[0]] * nc, axis=0)
    li = gi_ref[0, 0] + bias(gbi_ref)
    fpre = gf_ref[0, 0] + bias(gbf_ref)
    lf = jnp.minimum(fpre, 0.0) - jnp.log1p(jnp.exp(-jnp.abs(fpre)))
    direction = lax.broadcasted_iota(jnp.int32, li.shape, 0) % 8
    ui = lax.broadcasted_iota(jnp.int32, (CHUNK, CHUNK), 0)
    si = lax.broadcasted_iota(jnp.int32, (CHUNK, CHUNK), 1)
    cum = (_split3_dot(jnp.where(direction == 0, lf, 0.0), jnp.where(ui <= si, 1.0, 0.0).astype(BF16))
           + _split3_dot(jnp.where(direction == 1, lf, 0.0), jnp.where(ui >= si, 1.0, 0.0).astype(BF16)))
    a_tab = li - cum
    total = jnp.where(direction == 0, cum[:, CHUNK - 1:CHUNK], cum[:, 0:1])
    a_ref[...] = a_tab
    tot_ref[...] = jnp.broadcast_to(total, a_tab.shape)
    amx_ref[...] = jnp.broadcast_to(jnp.max(a_tab, axis=-1, keepdims=True), a_tab.shape)

    cum_ref[...] = cum

    def bcol_body(c, carry):
        rs = pl.ds(pl.multiple_of(c * CHUNK, CHUNK), CHUNK)
        tile = cum_ref[pl.ds(pl.multiple_of(c * 8, 8), 8), :]
        for d in range(2):
            bcol_ref[d, rs, :] = jnp.broadcast_to(tile[d:d + 1, :], (CHUNK, CHUNK)).T
        return carry

    lax.fori_loop(0, nc, bcol_body, 0, unroll=2)

    cf_ref[...] = jnp.zeros_like(cf_ref)
    cb_ref[...] = jnp.zeros_like(cb_ref)

    def step(c, c_ref, m, d):
        r0 = pl.multiple_of(c * CHUNK, CHUNK)
        g0 = pl.multiple_of(c * 8, 8)
        qc = qs_ref[pl.ds(r0, CHUNK), :]
        ktc = kt_ref[:, pl.ds(r0, CHUNK)]
        vc = va_ref[pl.ds(r0, CHUNK), :]
        a = a_ref[pl.ds(g0, 8), :][d:d + 1, :]
        tot = tot_ref[pl.ds(g0, 8), :][d:d + 1, :]
        amax = amx_ref[pl.ds(g0, 8), :][d:d + 1, :]
        visible = (ui >= si) if d == 0 else (ui <= si)
        am = jnp.where(visible, a, -jnp.inf)
        reach = jnp.broadcast_to(jnp.max(am, axis=-1, keepdims=True), (CHUNK, CHUNK))
        m_row = jnp.maximum(m, reach)
        inter = jnp.exp(m - m_row)
        s = _dot(qc, ktc) * jnp.exp(am - m_row)
        m_new = tot + jnp.maximum(m, amax)
        decay = jnp.exp(tot + m - m_new)
        kw = (ktc.astype(F32) * jnp.exp(tot + a - m_new)).astype(BF16)
        sv = _dot(jnp.concatenate([s.astype(BF16), kw], axis=0), vc)
        nd = sv[0:CHUNK, :] + jnp.concatenate([inter, inter], axis=1) * _dot(qc, c_ref[...].astype(BF16))
        clamp = jnp.exp(-(bcol_ref[d, pl.ds(r0, CHUNK), :] + m_row))
        h = nd[:, 0:dh] / jnp.maximum(jnp.abs(nd[:, dh:2 * dh]), clamp)
        c_ref[...] = jnp.concatenate([decay, decay], axis=1) * c_ref[...] + sv[CHUNK:2 * CHUNK, :]
        return h, m_new

    def scan_body(c, carry):
        m_f, m_b = carry
        h_f, m_f = step(c, cf_ref, m_f, 0)
        hf_ref[pl.ds(pl.multiple_of(c * CHUNK, CHUNK), CHUNK), :] = h_f
        cb = nc - 1 - c
        h_b, m_b = step(cb, cb_ref, m_b, 1)
        hb_ref[pl.ds(pl.multiple_of(cb * CHUNK, CHUNK), CHUNK), :] = h_b
        return m_f, m_b

    m_init = jnp.zeros((1, CHUNK), F32)
    lax.fori_loop(0, nc, scan_body, (m_init, m_init), unroll=8)

    def out_body(c, carry):
        rs = pl.ds(pl.multiple_of(c * CHUNK, CHUNK), CHUNK)
        y = _rms(hf_ref[rs, :] + hb_ref[rs, :], hg_ref[...])
        o_ref[0, rs, :] = (y * _sigmoid(oa_ref[0, rs, :])).astype(BF16)
        return carry

    lax.fori_loop(0, nc, out_body, 0, unroll=2)


def _mlstm(qk, va, oa, gi, gf, gbi, gbf, conv_w, head_g):
    bsz, seq, _ = qk.shape
    nc = seq // CHUNK
    dh = A_HEAD_DIM
    col = lambda off: pl.BlockSpec((1, seq, dh), lambda b, h: (b, 0, h + off))
    gate_spec = pl.BlockSpec((1, 1, 8 * nc, CHUNK), lambda b, h: (b, h, 0, 0))
    bias_spec = pl.BlockSpec((1, 8, 1), lambda b, h: (h, 0, 0))
    table = pltpu.VMEM((8 * nc, CHUNK), F32)
    return pl.pallas_call(
        _mlstm_kernel,
        grid=(bsz, A_HEADS),
        in_specs=[col(0), col(A_HEADS), col(0), col(0), gate_spec, gate_spec, bias_spec, bias_spec,
                  pl.BlockSpec((CONV_WIDTH, dh), lambda b, h: (0, h)),
                  pl.BlockSpec((CONV_WIDTH, dh), lambda b, h: (0, h + A_HEADS)),
                  pl.BlockSpec((1, dh), lambda b, h: (0, h))],
        out_specs=col(0),
        out_shape=jax.ShapeDtypeStruct((bsz, seq, A_WIDTH), BF16),
        scratch_shapes=[pltpu.VMEM((seq + 16, dh), F32),
                        pltpu.VMEM((seq, dh), BF16),
                        pltpu.VMEM((dh, seq), BF16),
                        pltpu.VMEM((seq, 2 * dh), BF16),
                        table,
                        table,
                        table,
                        table,
                        pltpu.VMEM((2, seq, CHUNK), F32),
                        pltpu.VMEM((seq, dh), F32),
                        pltpu.VMEM((seq, dh), F32),
                        pltpu.VMEM((dh, 2 * dh), F32),
                        pltpu.VMEM((dh, 2 * dh), F32)],
        compiler_params=_params(2),
        name="mlstm",
    )(qk, qk, va, oa, gi, gf, gbi, gbf, conv_w, conv_w, head_g)


def _rope_table_kernel(pos_ref, freq_ref, cos_ref, sin_ref, nsin_ref):
    ang = pos_ref[...].astype(F32) * freq_ref[...]
    sn = jnp.sin(ang)
    cos_ref[...] = jnp.cos(ang)
    sin_ref[...] = sn
    nsin_ref[...] = -sn


def _rope_tables(pos_rep, freq_rep):
    rows = pos_rep.shape[0]
    tr = min(rows, 512)
    spec = pl.BlockSpec((tr, LANES), lambda i: (i, 0))
    out = jax.ShapeDtypeStruct((rows, LANES), F32)
    return pl.pallas_call(
        _rope_table_kernel,
        grid=(rows // tr,),
        in_specs=[spec, _const_spec(freq_rep.shape)],
        out_specs=[spec, spec, spec],
        out_shape=[out, out, out],
        compiler_params=_params(1),
        name="rope_tables",
    )(pos_rep, freq_rep)


def _mla_in_kernel(x_ref, g_ref, w1_ref, qg_ref, wq_ref, kvg_ref, wk_ref, wv_ref, ct_ref, st_ref,
                   q_ref, k_ref, v_ref):
    hp = HEAD_PAD
    h = _rms(x_ref[...], g_ref[...]).astype(BF16)
    p1 = _dot(h, w1_ref[...])
    ct = ct_ref[...]
    st = st_ref[...]

    def rotary(a):
        partner = jnp.concatenate([a[hp // 2:, :], a[:hp // 2, :]], axis=0)
        return a * ct + partner * st

    k_rope = rotary(p1[:, Q_LORA + KV_LORA:].T).T

    cq = _rms(p1[:, :Q_LORA], qg_ref[...]).astype(BF16)
    qa = lax.dot_general(wq_ref[...], cq, (((1,), (1,)), ((), ())), preferred_element_type=F32)
    scale = (C_NOPE + C_ROPE) ** -0.5 * LOG2E
    for hd in range(C_HEADS):
        q_ref[hd * hp:(hd + 1) * hp, :] = (rotary(qa[hd * hp:(hd + 1) * hp, :]) * scale).astype(BF16)

    ckv = _rms(p1[:, Q_LORA:Q_LORA + KV_LORA], kvg_ref[...]).astype(BF16)
    kn = _dot(ckv, wk_ref[...])
    for hd in range(C_HEADS):
        k_ref[:, hd * hp:(hd + 1) * hp] = (kn[:, hd * hp:(hd + 1) * hp] + k_rope).astype(BF16)
    v_ref[...] = lax.dot_general(wv_ref[...], ckv, (((1,), (1,)), ((), ())),
                                 preferred_element_type=F32).astype(BF16)


def _mla_in(x2, g, w1, qg, wab, kvg, wk, wv, ct, st, tm=1024):
    n = x2.shape[0]
    row = lambda w: pl.BlockSpec((tm, w), lambda i: (i, 0))
    col = lambda h: pl.BlockSpec((h, tm), lambda i: (0, i))
    return pl.pallas_call(
        _mla_in_kernel,
        grid=(n // tm,),
        in_specs=[row(D_MODEL), _const_spec(g.shape), _const_spec(w1.shape), _const_spec(qg.shape),
                  _const_spec(wab.shape), _const_spec(kvg.shape), _const_spec(wk.shape),
                  _const_spec(wv.shape), col(HEAD_PAD), col(HEAD_PAD)],
        out_specs=[col(C_HEADS * HEAD_PAD), row(C_HEADS * HEAD_PAD), col(C_HEADS * C_V)],
        out_shape=[jax.ShapeDtypeStruct((C_HEADS * HEAD_PAD, n), BF16),
                   jax.ShapeDtypeStruct((n, C_HEADS * HEAD_PAD), BF16),
                   jax.ShapeDtypeStruct((C_HEADS * C_V, n), BF16)],
        compiler_params=_params(1),
        name="mla_in",
    )(x2, g, w1, qg, wab, kvg, wk, wv, ct, st)


def _mla_attn_kernel(qt_ref, k_ref, vt_ref, o_ref, vx_ref, sa_ref, sb_ref, *, tq):
    seq = k_ref.shape[1]
    hp = HEAD_PAD
    n_tiles = seq // tq
    n_blocks = seq // KEY_BLOCK
    n_items = ATTN_PAIRS * n_tiles
    ones_rows = V_EXT_ROWS - C_V
    for hd in range(2 * ATTN_PAIRS):
        vx_ref[hd, 0:C_V, :] = vt_ref[hd * C_V:(hd + 1) * C_V, :]
        vx_ref[hd, C_V:V_EXT_ROWS, :] = jnp.ones((ones_rows, seq), BF16)

    def keys(j):
        return slice(j * KEY_BLOCK, (j + 1) * KEY_BLOCK)

    def stage(score_item, score_ref, value_item, value_ref, m_prev):
        maxima = [None, None]
        accs = [None, None]
        if score_ref is not None:
            sp = score_item // n_tiles
            srows = pl.ds(pl.multiple_of((score_item % n_tiles) * tq, tq), tq)
            slanes = [pl.ds(pl.multiple_of((2 * sp + hh) * hp, hp), hp) for hh in range(2)]
            qts = [qt_ref[slanes[hh], srows] for hh in range(2)]
        if value_ref is not None:
            vp = value_item // n_tiles
            vrows = pl.ds(pl.multiple_of((value_item % n_tiles) * tq, tq), tq)
        for j in range(n_blocks):
            for hh in range(2):
                if score_ref is not None:
                    st = _dot(k_ref[0, keys(j), slanes[hh]], qts[hh])
                    score_ref[hh, keys(j), :] = st
                    m = jnp.max(st, axis=0, keepdims=True)
                    maxima[hh] = m if j == 0 else jnp.maximum(maxima[hh], m)
                if value_ref is not None:
                    pt = jnp.exp2(value_ref[hh, keys(j), :] - m_prev[hh]).astype(BF16)
                    o = _dot(vx_ref[2 * vp + hh, :, keys(j)], pt)
                    accs[hh] = o if j == 0 else accs[hh] + o
        if value_ref is not None:
            outs = [acc[0:C_V, :] / acc[C_V:C_V + 1, :] for acc in accs]
            olanes = pl.ds(pl.multiple_of(vp * 2 * C_V, 2 * C_V), 2 * C_V)
            o_ref[0, vrows, olanes] = jnp.concatenate(outs, axis=0).T.astype(BF16)
        return tuple(maxima)

    def pair_body(i, m_prev):
        m_mid = stage(2 * i + 1, sb_ref, 2 * i, sa_ref, m_prev)
        return stage(2 * i + 2, sa_ref, 2 * i + 1, sb_ref, m_mid)

    m_even = lax.fori_loop(0, n_items // 2 - 1, pair_body, stage(0, sa_ref, None, None, None))
    m_odd = stage(n_items - 1, sb_ref, n_items - 2, sa_ref, m_even)
    stage(None, None, n_items - 1, sb_ref, m_odd)


def _mla_attn(qt, k, vt, tq=256):
    bsz, seq, _ = k.shape
    heads = 2 * ATTN_PAIRS
    scores = pltpu.VMEM((2, seq, tq), F32)
    return pl.pallas_call(
        functools.partial(_mla_attn_kernel, tq=tq),
        grid=(bsz, C_HEADS // heads),
        in_specs=[pl.BlockSpec((heads * HEAD_PAD, seq), lambda b, h: (h, b)),
                  pl.BlockSpec((1, seq, heads * HEAD_PAD), lambda b, h: (b, 0, h)),
                  pl.BlockSpec((heads * C_V, seq), lambda b, h: (h, b))],
        out_specs=pl.BlockSpec((1, seq, heads * C_V), lambda b, h: (b, 0, h)),
        out_shape=jax.ShapeDtypeStruct((bsz, seq, C_HEADS * C_V), BF16),
        scratch_shapes=[pltpu.VMEM((heads, V_EXT_ROWS, seq), BF16),
                        scores, scores],
        compiler_params=_params(2),
        name="mla_attn",
    )(qt, k, vt)


def _mix_ffn_kernel(*refs, n_mix, tf, final):
    x_ref = refs[0]
    mix_refs = refs[1:1 + n_mix]
    wmix_ref, g_ref, w1_ref, w2_ref = refs[1 + n_mix:5 + n_mix]
    fg_ref = refs[5 + n_mix] if final else None
    o_ref = refs[-1]

    mix = jnp.concatenate([a_ref[...] for a_ref in mix_refs], axis=1)
    x = x_ref[...] + _dot(mix, wmix_ref[...])
    h = _rms(x, g_ref[...]).astype(BF16)
    for j in range(D_FF // tf):
        a = _dot(h, w1_ref[:, j * tf:(j + 1) * tf])
        a = jnp.square(jnp.maximum(a, 0.0)).astype(BF16)
        x = x + _dot(a, w2_ref[j * tf:(j + 1) * tf, :])
    if final:
        x = _rms(x, fg_ref[...])
    o_ref[...] = x


def _mix_ffn(x2, mixes, wmix, g, w1, w2, final_g=None, tm=1024, tf=512):
    n = x2.shape[0]
    row = lambda w: pl.BlockSpec((tm, w), lambda i: (i, 0))
    final = final_g is not None
    args = [x2, *mixes, wmix, g, w1, w2] + ([final_g] if final else [])
    in_specs = ([row(D_MODEL)] + [row(a.shape[1]) for a in mixes]
                + [_const_spec(a.shape) for a in args[1 + len(mixes):]])
    return pl.pallas_call(
        functools.partial(_mix_ffn_kernel, n_mix=len(mixes), tf=tf, final=final),
        grid=(n // tm,),
        in_specs=in_specs,
        out_specs=row(D_MODEL),
        out_shape=jax.ShapeDtypeStruct((n, D_MODEL), F32),
        compiler_params=_params(1),
        name="mix_ffn",
    )(*args)


def _pad_heads(w, width, parts):
    k = w.shape[0]
    w = w.reshape(k, C_HEADS, width)
    cols = [w[:, :, a:b] if a is not None else jnp.zeros((k, C_HEADS, b), w.dtype) for a, b in parts]
    used = sum(c.shape[2] for c in cols)
    cols.append(jnp.zeros((k, C_HEADS, HEAD_PAD - used), w.dtype))
    return jnp.concatenate(cols, axis=2).reshape(k, C_HEADS * HEAD_PAD)


def kernel(x, positions, ab_norm, ab_w_in, ab_conv, ab_gate_b, ab_head_g, ab_v_g, ab_ws, ab_bs,
           ab_w_out, c_norm, c_w_in, c_q_g, c_kv_g, c_w_uq, c_w_ukv, c_w_out, ffn_norm, ffn_w1,
           ffn_w2, final_norm):
    bsz, seq, _ = x.shape
    n = bsz * seq
    nc = seq // CHUNK
    half = C_ROPE // 2
    x2 = x.reshape(n, D_MODEL)

    freq = ROPE_BASE ** (-jnp.arange(half, dtype=F32) / half)
    pos_rep = jnp.repeat(positions.reshape(n), half).reshape(n * half // LANES, LANES)
    freq_rep = jnp.tile(freq, LANES // half).reshape(1, LANES)
    cos, sin, nsin = (t.reshape(n, half).T for t in _rope_tables(pos_rep, freq_rep))
    ones = lambda h: jnp.ones((h, n), F32)
    zeros = lambda h: jnp.zeros((h, n), F32)
    gap = HEAD_PAD // 2 - half
    cos_t = jnp.concatenate([cos, ones(gap), cos, ones(gap)], axis=0)
    sin_t = jnp.concatenate([nsin, zeros(gap), sin, zeros(gap)], axis=0)

    for layer in range(DEPTH):
        j = layer // 2
        final_g = final_norm.reshape(1, D_MODEL) if layer == DEPTH - 1 else None
        ffn_args = (ffn_norm[layer].reshape(1, D_MODEL), ffn_w1[layer].astype(BF16),
                    ffn_w2[layer].astype(BF16))
        if layer % 2 == 0:
            w = ab_w_in[j].astype(BF16)
            o0 = 4 * A_WIDTH
            o1 = o0 + 4 * A_HEADS
            qk, va, oa, gates, hb = _ab_in(
                x2, ab_norm[j].reshape(1, D_MODEL), w[:, :2 * A_WIDTH], w[:, 2 * A_WIDTH:3 * A_WIDTH],
                w[:, 3 * A_WIDTH:o0], w[:, o0:o1], w[:, o1:], ab_v_g[j].reshape(1, B_WIDTH),
                ab_ws[j].astype(BF16), ab_bs[j].T)
            gr = gates.reshape(bsz, nc, CHUNK, 2, 2, A_HEADS).transpose(4, 0, 5, 1, 3, 2)
            gr = jnp.pad(gr, ((0, 0),) * 4 + ((0, 6), (0, 0))).reshape(2, bsz, A_HEADS, 8 * nc, CHUNK)
            gb = jnp.pad(ab_gate_b[j].reshape(2, 2, A_HEADS).transpose(1, 2, 0), ((0, 0), (0, 0), (0, 6)))
            gb = gb.reshape(2, A_HEADS, 8, 1)
            ha = _mlstm(qk.reshape(bsz, seq, 2 * A_WIDTH), va.reshape(bsz, seq, A_WIDTH),
                        oa.reshape(bsz, seq, A_WIDTH), gr[0], gr[1], gb[0], gb[1], ab_conv[j],
                        ab_head_g[j].reshape(1, A_WIDTH))
            x2 = _mix_ffn(x2, [ha.reshape(n, A_WIDTH), hb], ab_w_out[j].astype(BF16),
                          *ffn_args, final_g=final_g)
        else:
            w_in = c_w_in[j]
            kr1 = w_in[:, Q_LORA + KV_LORA:Q_LORA + KV_LORA + half]
            kr2 = w_in[:, Q_LORA + KV_LORA + half:]
            zc = lambda w: jnp.zeros((D_MODEL, w), w_in.dtype)
            w1 = jnp.concatenate([w_in[:, :Q_LORA + KV_LORA], kr1, zc(gap), kr2, zc(gap)],
                                 axis=1).astype(BF16)
            qw = C_NOPE + C_ROPE
            wab = _pad_heads(c_w_uq[j], qw, [(C_NOPE, C_NOPE + half), (0, gap), (C_NOPE + half, qw),
                                             (gap, C_NOPE)]).T.astype(BF16)
            wk = _pad_heads(c_w_ukv[j], C_NOPE + C_V, [(None, half), (0, gap), (None, half),
                                                       (gap, C_NOPE)]).astype(BF16)
            wv = c_w_ukv[j].reshape(KV_LORA, C_HEADS, C_NOPE + C_V)[:, :, C_NOPE:]
            wv_t = wv.reshape(KV_LORA, C_HEADS * C_V).T.astype(BF16)
            qt, k, vt = _mla_in(x2, c_norm[j].reshape(1, D_MODEL), w1, c_q_g[j].reshape(1, Q_LORA), wab,
                                c_kv_g[j].reshape(1, KV_LORA), wk, wv_t, cos_t, sin_t)
            o = _mla_attn(qt, k.reshape(bsz, seq, -1), vt)
            x2 = _mix_ffn(x2, [o.reshape(n, C_HEADS * C_V)], c_w_out[j].astype(BF16),
                          *ffn_args, final_g=final_g)
    return x2.reshape(bsz, seq, D_MODEL)
```

```python
import functools

import jax
import jax.numpy as jnp
from jax import lax
from jax.experimental import pallas as pl
from jax.experimental.pallas import tpu as pltpu

F32 = jnp.float32
BF16 = jnp.bfloat16

D_MODEL = 1024
DEPTH = 4
A_WIDTH = 512
A_HEADS = 4
A_HEAD_DIM = 128
CHUNK = 128
CONV_WIDTH = 5
B_WIDTH = 512
B_GROUPS = 4
B_GROUP_DIM = 128
C_HEADS = 16
C_NOPE = 64
C_ROPE = 32
C_V = 64
Q_LORA = 384
KV_LORA = 256
ROPE_BASE = 10000.0
D_FF = 4 * D_MODEL
EPS = 1e-6

LANES = 128
HEAD_PAD = 128
V_EXT_ROWS = C_V + 16
LOG2E = 1.4426950408889634
KEY_BLOCK = 256
ATTN_PAIRS = 2
VMEM_LIMIT = 56 * 1024 * 1024


def _params(n_axes):
    return pltpu.CompilerParams(
        dimension_semantics=("arbitrary",) * n_axes, vmem_limit_bytes=VMEM_LIMIT)


def _const_spec(shape):
    nd = len(shape)
    return pl.BlockSpec(shape, lambda *_: (0,) * nd, pipeline_mode=pl.Buffered(1))


def _rms(x, g):
    return x * lax.rsqrt(jnp.mean(x * x, axis=-1, keepdims=True) + EPS) * g


def _dot(a, b):
    return jnp.dot(a, b, preferred_element_type=F32)


def _sigmoid(x):
    return 1.0 / (1.0 + jnp.exp(-x))


def _ab_in_kernel(x_ref, g_ref, wqk_ref, wv_ref, wo_ref, wg_ref, gb_ref, wuv_ref, vg_ref, ws_ref, bs_ref,
                  qk_ref, va_ref, oa_ref, gt_ref, hb_ref):
    tm = x_ref.shape[0]
    h = _rms(x_ref[...], g_ref[...]).astype(BF16)
    qk_ref[...] = _dot(h, wqk_ref[...])
    va_ref[...] = _dot(h, wv_ref[...]).astype(BF16)
    oa_ref[...] = _dot(h, wo_ref[...])
    gt_ref[...] = (_dot(h, wg_ref[...]) + gb_ref[...]).T[0:4 * A_HEADS, :]

    uv = _dot(h, wuv_ref[...])
    uv = 0.5 * uv * (1.0 + jnp.tanh(0.7978845608028654 * (uv + 0.044715 * (uv * uv * uv))))
    n_chunks = tm // CHUNK
    for g in range(B_GROUPS):
        lo = g * B_GROUP_DIM
        u = uv[:, lo:lo + B_GROUP_DIM]
        vb = uv[:, B_WIDTH + lo:B_WIDTH + lo + B_GROUP_DIM]
        vb = _rms(vb, vg_ref[:, lo:lo + B_GROUP_DIM]).astype(BF16)
        rhs = jnp.concatenate([vb[c * CHUNK:(c + 1) * CHUNK, :] for c in range(n_chunks)], axis=1)
        sp = _dot(ws_ref[g], rhs) + bs_ref[:, g:g + 1]
        for c in range(n_chunks):
            hb_ref[c * CHUNK:(c + 1) * CHUNK, lo:lo + B_GROUP_DIM] = (
                u[c * CHUNK:(c + 1) * CHUNK, :] * sp[:, c * B_GROUP_DIM:(c + 1) * B_GROUP_DIM]
            ).astype(BF16)


def _ab_in(x2, g, wqk, wv, wo, wg, gb, wuv, vg, ws, bs_t, tm=1024):
    n = x2.shape[0]
    row = lambda w: pl.BlockSpec((tm, w), lambda i: (i, 0))
    return pl.pallas_call(
        _ab_in_kernel,
        grid=(n // tm,),
        in_specs=[row(D_MODEL), _const_spec(g.shape), _const_spec(wqk.shape), _const_spec(wv.shape),
                  _const_spec(wo.shape), _const_spec(wg.shape), _const_spec(gb.shape),
                  _const_spec(wuv.shape), _const_spec(vg.shape), _const_spec(ws.shape),
                  _const_spec(bs_t.shape)],
        out_specs=[row(2 * A_WIDTH), row(A_WIDTH), row(A_WIDTH),
                   pl.BlockSpec((4 * A_HEADS, tm), lambda i: (0, i)), row(B_WIDTH)],
        out_shape=[jax.ShapeDtypeStruct((n, 2 * A_WIDTH), F32),
                   jax.ShapeDtypeStruct((n, A_WIDTH), BF16),
                   jax.ShapeDtypeStruct((n, A_WIDTH), F32),
                   jax.ShapeDtypeStruct((4 * A_HEADS, n), F32),
                   jax.ShapeDtypeStruct((n, B_WIDTH), BF16)],
        compiler_params=_params(1),
        name="ab_in",
    )(x2, g, wqk, wv, wo, wg, gb, wuv, vg, ws, bs_t)


def _split3_dot(x, tri):
    x0 = x.astype(BF16)
    r1 = x - x0.astype(F32)
    x1 = r1.astype(BF16)
    x2 = (r1 - x1.astype(F32)).astype(BF16)
    return _dot(x0, tri) + _dot(x1, tri) + _dot(x2, tri)


def _mlstm_kernel(q_ref, k_ref, v_ref, oa_ref, gt_ref, cwq_ref, cwk_ref, hg_ref, o_ref,
                  xp_ref, qs_ref, kt_ref, va_ref, a_ref, cum_ref, tot_ref, amx_ref, bcol_ref,
                  hf_ref, hb_ref, cf_ref, cb_ref):
    seq = q_ref.shape[1]
    nc = seq // CHUNK
    dh = A_HEAD_DIM
    halo = 8
    pad = (CONV_WIDTH - 1) // 2

    zeros_halo = jnp.zeros((halo, dh), F32)
    xp_ref[0:halo, :] = zeros_halo
    xp_ref[halo + seq:halo + seq + halo, :] = zeros_halo

    def conv_chunk(c, w):
        r0 = pl.multiple_of(c * CHUNK, CHUNK) + (halo - pad)
        y = xp_ref[pl.ds(r0, CHUNK), :] * w[0:1, :]
        for j in range(1, CONV_WIDTH):
            y = y + xp_ref[pl.ds(r0 + j, CHUNK), :] * w[j:j + 1, :]
        return y * _sigmoid(y)

    xp_ref[halo:halo + seq, :] = q_ref[0]
    wq = cwq_ref[...]

    def q_body(c, carry):
        r0 = pl.multiple_of(c * CHUNK, CHUNK)
        qs_ref[pl.ds(r0, CHUNK), :] = (conv_chunk(c, wq) * (dh ** -0.5)).astype(BF16)
        return carry

    lax.fori_loop(0, nc, q_body, 0, unroll=2)

    xp_ref[halo:halo + seq, :] = k_ref[0]
    wk = cwk_ref[...]

    def k_body(c, carry):
        r0 = pl.multiple_of(c * CHUNK, CHUNK)
        kt_ref[:, pl.ds(r0, CHUNK)] = conv_chunk(c, wk).T.astype(BF16)
        return carry

    lax.fori_loop(0, nc, k_body, 0, unroll=2)

    va_ref[:, 0:dh] = v_ref[0]
    va_ref[:, dh:2 * dh] = jnp.ones((seq, dh), BF16)

    a_ref[...] = jnp.zeros_like(a_ref)
    cum_ref[...] = jnp.zeros_like(cum_ref)
    head = pl.program_id(1)
    for kind, (table, d) in enumerate([(a_ref, 0), (cum_ref, 0), (a_ref, 1), (cum_ref, 1)]):
        gate = gt_ref[pl.ds(head * 4 + kind, 1), :]
        for c in range(nc):
            table[8 * c + d:8 * c + d + 1, :] = gate[:, c * CHUNK:(c + 1) * CHUNK]
    li = a_ref[...]
    fpre = cum_ref[...]
    lf = jnp.minimum(fpre, 0.0) - jnp.log1p(jnp.exp(-jnp.abs(fpre)))
    direction = lax.broadcasted_iota(jnp.int32, li.shape, 0) % 8
    ui = lax.broadcasted_iota(jnp.int32, (CHUNK, CHUNK), 0)
    si = lax.broadcasted_iota(jnp.int32, (CHUNK, CHUNK), 1)
    cum = (_split3_dot(jnp.where(direction == 0, lf, 0.0), jnp.where(ui <= si, 1.0, 0.0).astype(BF16))
           + _split3_dot(jnp.where(direction == 1, lf, 0.0), jnp.where(ui >= si, 1.0, 0.0).astype(BF16)))
    a_tab = li - cum
    total = jnp.where(direction == 0, cum[:, CHUNK - 1:CHUNK], cum[:, 0:1])
    a_ref[...] = a_tab
    tot_ref[...] = jnp.broadcast_to(total, a_tab.shape)
    amx_ref[...] = jnp.broadcast_to(jnp.max(a_tab, axis=-1, keepdims=True), a_tab.shape)

    cum_ref[...] = cum

    def bcol_body(c, carry):
        rs = pl.ds(pl.multiple_of(c * CHUNK, CHUNK), CHUNK)
        tile = cum_ref[pl.ds(pl.multiple_of(c * 8, 8), 8), :]
        for d in range(2):
            bcol_ref[d, rs, :] = jnp.broadcast_to(tile[d:d + 1, :], (CHUNK, CHUNK)).T
        return carry

    lax.fori_loop(0, nc, bcol_body, 0, unroll=2)

    cf_ref[...] = jnp.zeros_like(cf_ref)
    cb_ref[...] = jnp.zeros_like(cb_ref)

    def step(c, c_ref, m, d):
        r0 = pl.multiple_of(c * CHUNK, CHUNK)
        g0 = pl.multiple_of(c * 8, 8)
        qc = qs_ref[pl.ds(r0, CHUNK), :]
        ktc = kt_ref[:, pl.ds(r0, CHUNK)]
        vc = va_ref[pl.ds(r0, CHUNK), :]
        a = a_ref[pl.ds(g0, 8), :][d:d + 1, :]
        tot = tot_ref[pl.ds(g0, 8), :][d:d + 1, :]
        amax = amx_ref[pl.ds(g0, 8), :][d:d + 1, :]
        visible = (ui >= si) if d == 0 else (ui <= si)
        am = jnp.where(visible, a, -jnp.inf)
        reach = jnp.broadcast_to(jnp.max(am, axis=-1, keepdims=True), (CHUNK, CHUNK))
        m_row = jnp.maximum(m, reach)
        inter = jnp.exp(m - m_row)
        s = _dot(qc, ktc) * jnp.exp(am - m_row)
        m_new = tot + jnp.maximum(m, amax)
        decay = jnp.exp(tot + m - m_new)
        kw = (ktc.astype(F32) * jnp.exp(tot + a - m_new)).astype(BF16)
        sv = _dot(jnp.concatenate([s.astype(BF16), kw], axis=0), vc)
        nd = sv[0:CHUNK, :] + jnp.concatenate([inter, inter], axis=1) * _dot(qc, c_ref[...].astype(BF16))
        clamp = jnp.exp(-(bcol_ref[d, pl.ds(r0, CHUNK), :] + m_row))
        h = nd[:, 0:dh] / jnp.maximum(jnp.abs(nd[:, dh:2 * dh]), clamp)
        c_ref[...] = jnp.concatenate([decay, decay], axis=1) * c_ref[...] + sv[CHUNK:2 * CHUNK, :]
        return h, m_new

    def scan_body(c, carry):
        m_f, m_b = carry
        h_f, m_f = step(c, cf_ref, m_f, 0)
        hf_ref[pl.ds(pl.multiple_of(c * CHUNK, CHUNK), CHUNK), :] = h_f
        cb = nc - 1 - c
        h_b, m_b = step(cb, cb_ref, m_b, 1)
        hb_ref[pl.ds(pl.multiple_of(cb * CHUNK, CHUNK), CHUNK), :] = h_b
        return m_f, m_b

    m_init = jnp.zeros((1, CHUNK), F32)
    lax.fori_loop(0, nc, scan_body, (m_init, m_init), unroll=8)

    def out_body(c, carry):
        rs = pl.ds(pl.multiple_of(c * CHUNK, CHUNK), CHUNK)
        y = _rms(hf_ref[rs, :] + hb_ref[rs, :], hg_ref[...])
        o_ref[0, rs, :] = (y * _sigmoid(oa_ref[0, rs, :])).astype(BF16)
        return carry

    lax.fori_loop(0, nc, out_body, 0, unroll=2)


def _mlstm(qk, va, oa, gt, conv_w, head_g):
    bsz, seq, _ = qk.shape
    nc = seq // CHUNK
    dh = A_HEAD_DIM
    col = lambda off: pl.BlockSpec((1, seq, dh), lambda b, h: (b, 0, h + off))
    table = pltpu.VMEM((8 * nc, CHUNK), F32)
    return pl.pallas_call(
        _mlstm_kernel,
        grid=(bsz, A_HEADS),
        in_specs=[col(0), col(A_HEADS), col(0), col(0),
                  pl.BlockSpec((4 * A_HEADS, seq), lambda b, h: (0, b)),
                  pl.BlockSpec((CONV_WIDTH, dh), lambda b, h: (0, h)),
                  pl.BlockSpec((CONV_WIDTH, dh), lambda b, h: (0, h + A_HEADS)),
                  pl.BlockSpec((1, dh), lambda b, h: (0, h))],
        out_specs=col(0),
        out_shape=jax.ShapeDtypeStruct((bsz, seq, A_WIDTH), BF16),
        scratch_shapes=[pltpu.VMEM((seq + 16, dh), F32),
                        pltpu.VMEM((seq, dh), BF16),
                        pltpu.VMEM((dh, seq), BF16),
                        pltpu.VMEM((seq, 2 * dh), BF16),
                        table,
                        table,
                        table,
                        table,
                        pltpu.VMEM((2, seq, CHUNK), F32),
                        pltpu.VMEM((seq, dh), F32),
                        pltpu.VMEM((seq, dh), F32),
                        pltpu.VMEM((dh, 2 * dh), F32),
                        pltpu.VMEM((dh, 2 * dh), F32)],
        compiler_params=_params(2),
        name="mlstm",
    )(qk, qk, va, oa, gt, conv_w, conv_w, head_g)


def _rope_table_kernel(pos_ref, freq_ref, cos_ref, sin_ref, nsin_ref):
    ang = pos_ref[...].astype(F32) * freq_ref[...]
    sn = jnp.sin(ang)
    cos_ref[...] = jnp.cos(ang)
    sin_ref[...] = sn
    nsin_ref[...] = -sn


def _rope_tables(pos_rep, freq_rep):
    rows = pos_rep.shape[0]
    tr = min(rows, 512)
    spec = pl.BlockSpec((tr, LANES), lambda i: (i, 0))
    out = jax.ShapeDtypeStruct((rows, LANES), F32)
    return pl.pallas_call(
        _rope_table_kernel,
        grid=(rows // tr,),
        in_specs=[spec, _const_spec(freq_rep.shape)],
        out_specs=[spec, spec, spec],
        out_shape=[out, out, out],
        compiler_params=_params(1),
        name="rope_tables",
    )(pos_rep, freq_rep)


def _mla_in_kernel(x_ref, g_ref, w1_ref, qg_ref, wq_ref, kvg_ref, wk_ref, wv_ref, ct_ref, st_ref,
                   q_ref, k_ref, v_ref):
    hp = HEAD_PAD
    h = _rms(x_ref[...], g_ref[...]).astype(BF16)
    p1 = _dot(h, w1_ref[...])
    ct = ct_ref[...]
    st = st_ref[...]

    def rotary(a):
        partner = jnp.concatenate([a[hp // 2:, :], a[:hp // 2, :]], axis=0)
        return a * ct + partner * st

    k_rope = rotary(p1[:, Q_LORA + KV_LORA:].T).T

    cq = _rms(p1[:, :Q_LORA], qg_ref[...]).astype(BF16)
    qa = lax.dot_general(wq_ref[...], cq, (((1,), (1,)), ((), ())), preferred_element_type=F32)
    scale = (C_NOPE + C_ROPE) ** -0.5 * LOG2E
    for hd in range(C_HEADS):
        q_ref[hd * hp:(hd + 1) * hp, :] = (rotary(qa[hd * hp:(hd + 1) * hp, :]) * scale).astype(BF16)

    ckv = _rms(p1[:, Q_LORA:Q_LORA + KV_LORA], kvg_ref[...]).astype(BF16)
    kn = _dot(ckv, wk_ref[...])
    for hd in range(C_HEADS):
        k_ref[:, hd * hp:(hd + 1) * hp] = (kn[:, hd * hp:(hd + 1) * hp] + k_rope).astype(BF16)
    v_ref[...] = lax.dot_general(wv_ref[...], ckv, (((1,), (1,)), ((), ())),
                                 preferred_element_type=F32).astype(BF16)


def _mla_in(x2, g, w1, qg, wab, kvg, wk, wv, ct, st, tm=1024):
    n = x2.shape[0]
    row = lambda w: pl.BlockSpec((tm, w), lambda i: (i, 0))
    col = lambda h: pl.BlockSpec((h, tm), lambda i: (0, i))
    return pl.pallas_call(
        _mla_in_kernel,
        grid=(n // tm,),
        in_specs=[row(D_MODEL), _const_spec(g.shape), _const_spec(w1.shape), _const_spec(qg.shape),
                  _const_spec(wab.shape), _const_spec(kvg.shape), _const_spec(wk.shape),
                  _const_spec(wv.shape), col(HEAD_PAD), col(HEAD_PAD)],
        out_specs=[col(C_HEADS * HEAD_PAD), row(C_HEADS * HEAD_PAD), col(C_HEADS * C_V)],
        out_shape=[jax.ShapeDtypeStruct((C_HEADS * HEAD_PAD, n), BF16),
                   jax.ShapeDtypeStruct((n, C_HEADS * HEAD_PAD), BF16),
                   jax.ShapeDtypeStruct((C_HEADS * C_V, n), BF16)],
        compiler_params=_params(1),
        name="mla_in",
    )(x2, g, w1, qg, wab, kvg, wk, wv, ct, st)


def _mla_attn_kernel(qt_ref, k_ref, vt_ref, o_ref, vx_ref, sa_ref, sb_ref, *, tq):
    seq = k_ref.shape[1]
    hp = HEAD_PAD
    n_tiles = seq // tq
    n_blocks = seq // KEY_BLOCK
    n_items = ATTN_PAIRS * n_tiles
    ones_rows = V_EXT_ROWS - C_V
    for hd in range(2 * ATTN_PAIRS):
        vx_ref[hd, 0:C_V, :] = vt_ref[hd * C_V:(hd + 1) * C_V, :]
        vx_ref[hd, C_V:V_EXT_ROWS, :] = jnp.ones((ones_rows, seq), BF16)

    def keys(j):
        return slice(j * KEY_BLOCK, (j + 1) * KEY_BLOCK)

    def stage(score_item, score_ref, value_item, value_ref, m_prev):
        maxima = [None, None]
        accs = [None, None]
        if score_ref is not None:
            sp = score_item // n_tiles
            srows = pl.ds(pl.multiple_of((score_item % n_tiles) * tq, tq), tq)
            slanes = [pl.ds(pl.multiple_of((2 * sp + hh) * hp, hp), hp) for hh in range(2)]
            qts = [qt_ref[slanes[hh], srows] for hh in range(2)]
        if value_ref is not None:
            vp = value_item // n_tiles
            vrows = pl.ds(pl.multiple_of((value_item % n_tiles) * tq, tq), tq)
        for j in range(n_blocks):
            for hh in range(2):
                if score_ref is not None:
                    st = _dot(k_ref[0, keys(j), slanes[hh]], qts[hh])
                    score_ref[hh, keys(j), :] = st
                    m = jnp.max(st, axis=0, keepdims=True)
                    maxima[hh] = m if j == 0 else jnp.maximum(maxima[hh], m)
                if value_ref is not None:
                    pt = jnp.exp2(value_ref[hh, keys(j), :] - m_prev[hh]).astype(BF16)
                    o = _dot(vx_ref[2 * vp + hh, :, keys(j)], pt)
                    accs[hh] = o if j == 0 else accs[hh] + o
        if value_ref is not None:
            outs = [acc[0:C_V, :] / acc[C_V:C_V + 1, :] for acc in accs]
            olanes = pl.ds(pl.multiple_of(vp * 2 * C_V, 2 * C_V), 2 * C_V)
            o_ref[0, vrows, olanes] = jnp.concatenate(outs, axis=0).T.astype(BF16)
        return tuple(maxima)

    def pair_body(i, m_prev):
        m_mid = stage(2 * i + 1, sb_ref, 2 * i, sa_ref, m_prev)
        return stage(2 * i + 2, sa_ref, 2 * i + 1, sb_ref, m_mid)

    m_even = lax.fori_loop(0, n_items // 2 - 1, pair_body, stage(0, sa_ref, None, None, None))
    m_odd = stage(n_items - 1, sb_ref, n_items - 2, sa_ref, m_even)
    stage(None, None, n_items - 1, sb_ref, m_odd)


def _mla_attn(qt, k, vt, tq=256):
    bsz, seq, _ = k.shape
    heads = 2 * ATTN_PAIRS
    scores = pltpu.VMEM((2, seq, tq), F32)
    return pl.pallas_call(
        functools.partial(_mla_attn_kernel, tq=tq),
        grid=(bsz, C_HEADS // heads),
        in_specs=[pl.BlockSpec((heads * HEAD_PAD, seq), lambda b, h: (h, b)),
                  pl.BlockSpec((1, seq, heads * HEAD_PAD), lambda b, h: (b, 0, h)),
                  pl.BlockSpec((heads * C_V, seq), lambda b, h: (h, b))],
        out_specs=pl.BlockSpec((1, seq, heads * C_V), lambda b, h: (b, 0, h)),
        out_shape=jax.ShapeDtypeStruct((bsz, seq, C_HEADS * C_V), BF16),
        scratch_shapes=[pltpu.VMEM((heads, V_EXT_ROWS, seq), BF16),
                        scores, scores],
        compiler_params=_params(2),
        name="mla_attn",
    )(qt, k, vt)


def _mix_ffn_kernel(*refs, n_mix, tf, final):
    x_ref = refs[0]
    mix_refs = refs[1:1 + n_mix]
    wmix_ref, g_ref, w1_ref, w2_ref = refs[1 + n_mix:5 + n_mix]
    fg_ref = refs[5 + n_mix] if final else None
    o_ref = refs[-1]

    mix = jnp.concatenate([a_ref[...] for a_ref in mix_refs], axis=1)
    x = x_ref[...] + _dot(mix, wmix_ref[...])
    h = _rms(x, g_ref[...]).astype(BF16)
    for j in range(D_FF // tf):
        a = _dot(h, w1_ref[:, j * tf:(j + 1) * tf])
        a = jnp.square(jnp.maximum(a, 0.0)).astype(BF16)
        x = x + _dot(a, w2_ref[j * tf:(j + 1) * tf, :])
    if final:
        x = _rms(x, fg_ref[...])
    o_ref[...] = x


def _mix_ffn(x2, mixes, wmix, g, w1, w2, final_g=None, tm=1024, tf=512):
    n = x2.shape[0]
    row = lambda w: pl.BlockSpec((tm, w), lambda i: (i, 0))
    final = final_g is not None
    args = [x2, *mixes, wmix, g, w1, w2] + ([final_g] if final else [])
    in_specs = ([row(D_MODEL)] + [row(a.shape[1]) for a in mixes]
                + [_const_spec(a.shape) for a in args[1 + len(mixes):]])
    return pl.pallas_call(
        functools.partial(_mix_ffn_kernel, n_mix=len(mixes), tf=tf, final=final),
        grid=(n // tm,),
        in_specs=in_specs,
        out_specs=row(D_MODEL),
        out_shape=jax.ShapeDtypeStruct((n, D_MODEL), F32),
        compiler_params=_params(1),
        name="mix_ffn",
    )(*args)


def _pad_heads(w, width, parts):
    k = w.shape[0]
    w = w.reshape(k, C_HEADS, width)
    cols = [w[:, :, a:b] if a is not None else jnp.zeros((k, C_HEADS, b), w.dtype) for a, b in parts]
    used = sum(c.shape[2] for c in cols)
    cols.append(jnp.zeros((k, C_HEADS, HEAD_PAD - used), w.dtype))
    return jnp.concatenate(cols, axis=2).reshape(k, C_HEADS * HEAD_PAD)


def kernel(x, positions, ab_norm, ab_w_in, ab_conv, ab_gate_b, ab_head_g, ab_v_g, ab_ws, ab_bs,
           ab_w_out, c_norm, c_w_in, c_q_g, c_kv_g, c_w_uq, c_w_ukv, c_w_out, ffn_norm, ffn_w1,
           ffn_w2, final_norm):
    bsz, seq, _ = x.shape
    n = bsz * seq
    nc = seq // CHUNK
    half = C_ROPE // 2
    x2 = x.reshape(n, D_MODEL)

    freq = ROPE_BASE ** (-jnp.arange(half, dtype=F32) / half)
    pos_rep = jnp.repeat(positions.reshape(n), half).reshape(n * half // LANES, LANES)
    freq_rep = jnp.tile(freq, LANES // half).reshape(1, LANES)
    cos, sin, nsin = (t.reshape(n, half).T for t in _rope_tables(pos_rep, freq_rep))
    ones = lambda h: jnp.ones((h, n), F32)
    zeros = lambda h: jnp.zeros((h, n), F32)
    gap = HEAD_PAD // 2 - half
    cos_t = jnp.concatenate([cos, ones(gap), cos, ones(gap)], axis=0)
    sin_t = jnp.concatenate([nsin, zeros(gap), sin, zeros(gap)], axis=0)

    for layer in range(DEPTH):
        j = layer // 2
        final_g = final_norm.reshape(1, D_MODEL) if layer == DEPTH - 1 else None
        ffn_args = (ffn_norm[layer].reshape(1, D_MODEL), ffn_w1[layer].astype(BF16),
                    ffn_w2[layer].astype(BF16))
        if layer % 2 == 0:
            w = ab_w_in[j].astype(BF16)
            o0 = 4 * A_WIDTH
            o1 = o0 + 4 * A_HEADS
            wg = w[:, o0:o1].reshape(D_MODEL, 4, A_HEADS).transpose(0, 2, 1).reshape(D_MODEL, 4 * A_HEADS)
            wg = jnp.pad(wg, ((0, 0), (0, LANES - 4 * A_HEADS)))
            gb = jnp.pad(ab_gate_b[j].reshape(4, A_HEADS).T.reshape(1, 4 * A_HEADS),
                         ((0, 0), (0, LANES - 4 * A_HEADS)))
            qk, va, oa, gates_t, hb = _ab_in(
                x2, ab_norm[j].reshape(1, D_MODEL), w[:, :2 * A_WIDTH], w[:, 2 * A_WIDTH:3 * A_WIDTH],
                w[:, 3 * A_WIDTH:o0], wg, gb, w[:, o1:], ab_v_g[j].reshape(1, B_WIDTH),
                ab_ws[j].astype(BF16), ab_bs[j].T)
            ha = _mlstm(qk.reshape(bsz, seq, 2 * A_WIDTH), va.reshape(bsz, seq, A_WIDTH),
                        oa.reshape(bsz, seq, A_WIDTH), gates_t, ab_conv[j],
                        ab_head_g[j].reshape(1, A_WIDTH))
            x2 = _mix_ffn(x2, [ha.reshape(n, A_WIDTH), hb], ab_w_out[j].astype(BF16),
                          *ffn_args, final_g=final_g)
        else:
            w_in = c_w_in[j]
            kr1 = w_in[:, Q_LORA + KV_LORA:Q_LORA + KV_LORA + half]
            kr2 = w_in[:, Q_LORA + KV_LORA + half:]
            zc = lambda w: jnp.zeros((D_MODEL, w), w_in.dtype)
            w1 = jnp.concatenate([w_in[:, :Q_LORA + KV_LORA], kr1, zc(gap), kr2, zc(gap)],
                                 axis=1).astype(BF16)
            qw = C_NOPE + C_ROPE
            wab = _pad_heads(c_w_uq[j], qw, [(C_NOPE, C_NOPE + half), (0, gap), (C_NOPE + half, qw),
                                             (gap, C_NOPE)]).T.astype(BF16)
            wk = _pad_heads(c_w_ukv[j], C_NOPE + C_V, [(None, half), (0, gap), (None, half),
                                                       (gap, C_NOPE)]).astype(BF16)
            wv = c_w_ukv[j].reshape(KV_LORA, C_HEADS, C_NOPE + C_V)[:, :, C_NOPE:]
            wv_t = wv.reshape(KV_LORA, C_HEADS * C_V).T.astype(BF16)
            qt, k, vt = _mla_in(x2, c_norm[j].reshape(1, D_MODEL), w1, c_q_g[j].reshape(1, Q_LORA), wab,
                                c_kv_g[j].reshape(1, KV_LORA), wk, wv_t, cos_t, sin_t)
            o = _mla_attn(qt, k.reshape(bsz, seq, -1), vt)
            x2 = _mix_ffn(x2, [o.reshape(n, C_HEADS * C_V)], c_w_out[j].astype(BF16),
                          *ffn_args, final_g=final_g)
    return x2.reshape(bsz, seq, D_MODEL)
```

```python
import functools

import jax
import jax.numpy as jnp
from jax import lax
from jax.experimental import pallas as pl
from jax.experimental.pallas import tpu as pltpu

F32 = jnp.float32
BF16 = jnp.bfloat16

D_MODEL = 1024
DEPTH = 4
A_WIDTH = 512
A_HEADS = 4
A_HEAD_DIM = 128
CHUNK = 128
CONV_WIDTH = 5
B_WIDTH = 512
B_GROUPS = 4
B_GROUP_DIM = 128
C_HEADS = 16
C_NOPE = 64
C_ROPE = 32
C_V = 64
Q_LORA = 384
KV_LORA = 256
ROPE_BASE = 10000.0
D_FF = 4 * D_MODEL
EPS = 1e-6

LANES = 128
HEAD_PAD = 128
V_EXT_ROWS = C_V + 16
LOG2E = 1.4426950408889634
KEY_BLOCK = 256
ATTN_PAIRS = 2
VMEM_LIMIT = 56 * 1024 * 1024


def _params(n_axes):
    return pltpu.CompilerParams(
        dimension_semantics=("arbitrary",) * n_axes, vmem_limit_bytes=VMEM_LIMIT)


def _const_spec(shape):
    nd = len(shape)
    return pl.BlockSpec(shape, lambda *_: (0,) * nd, pipeline_mode=pl.Buffered(1))


def _rms(x, g):
    return x * lax.rsqrt(jnp.mean(x * x, axis=-1, keepdims=True) + EPS) * g


def _dot(a, b):
    return jnp.dot(a, b, preferred_element_type=F32)


def _sigmoid(x):
    return 1.0 / (1.0 + jnp.exp(-x))


def _ab_in_kernel(x_ref, g_ref, wqk_ref, wv_ref, wo_ref, wg_ref, gb_ref, wuv_ref, vg_ref, ws_ref, bs_ref,
                  qk_ref, va_ref, oa_ref, gt_ref, hb_ref):
    tm = x_ref.shape[0]
    h = _rms(x_ref[...], g_ref[...]).astype(BF16)
    qk_ref[...] = _dot(h, wqk_ref[...])
    va_ref[...] = _dot(h, wv_ref[...]).astype(BF16)
    oa_ref[...] = _dot(h, wo_ref[...])
    gt_ref[...] = (_dot(h, wg_ref[...]) + gb_ref[...]).T[0:4 * A_HEADS, :]

    uv = _dot(h, wuv_ref[...])
    uv = 0.5 * uv * (1.0 + jnp.tanh(0.7978845608028654 * (uv + 0.044715 * (uv * uv * uv))))
    n_chunks = tm // CHUNK
    for g in range(B_GROUPS):
        lo = g * B_GROUP_DIM
        u = uv[:, lo:lo + B_GROUP_DIM]
        vb = uv[:, B_WIDTH + lo:B_WIDTH + lo + B_GROUP_DIM]
        vb = _rms(vb, vg_ref[:, lo:lo + B_GROUP_DIM]).astype(BF16)
        rhs = jnp.concatenate([vb[c * CHUNK:(c + 1) * CHUNK, :] for c in range(n_chunks)], axis=1)
        sp = _dot(ws_ref[g], rhs) + bs_ref[:, g:g + 1]
        for c in range(n_chunks):
            hb_ref[c * CHUNK:(c + 1) * CHUNK, lo:lo + B_GROUP_DIM] = (
                u[c * CHUNK:(c + 1) * CHUNK, :] * sp[:, c * B_GROUP_DIM:(c + 1) * B_GROUP_DIM]
            ).astype(BF16)


def _ab_in(x2, g, wqk, wv, wo, wg, gb, wuv, vg, ws, bs_t, tm=1024):
    n = x2.shape[0]
    row = lambda w: pl.BlockSpec((tm, w), lambda i: (i, 0))
    return pl.pallas_call(
        _ab_in_kernel,
        grid=(n // tm,),
        in_specs=[row(D_MODEL), _const_spec(g.shape), _const_spec(wqk.shape), _const_spec(wv.shape),
                  _const_spec(wo.shape), _const_spec(wg.shape), _const_spec(gb.shape),
                  _const_spec(wuv.shape), _const_spec(vg.shape), _const_spec(ws.shape),
                  _const_spec(bs_t.shape)],
        out_specs=[row(2 * A_WIDTH), row(A_WIDTH), row(A_WIDTH),
                   pl.BlockSpec((4 * A_HEADS, tm), lambda i: (0, i)), row(B_WIDTH)],
        out_shape=[jax.ShapeDtypeStruct((n, 2 * A_WIDTH), F32),
                   jax.ShapeDtypeStruct((n, A_WIDTH), BF16),
                   jax.ShapeDtypeStruct((n, A_WIDTH), F32),
                   jax.ShapeDtypeStruct((4 * A_HEADS, n), F32),
                   jax.ShapeDtypeStruct((n, B_WIDTH), BF16)],
        compiler_params=_params(1),
        name="ab_in",
    )(x2, g, wqk, wv, wo, wg, gb, wuv, vg, ws, bs_t)


def _split3_dot(x, tri):
    x0 = x.astype(BF16)
    r1 = x - x0.astype(F32)
    x1 = r1.astype(BF16)
    x2 = (r1 - x1.astype(F32)).astype(BF16)
    return _dot(x0, tri) + _dot(x1, tri) + _dot(x2, tri)


def _mlstm_kernel(q_ref, k_ref, v_ref, oa_ref, gt_ref, cwq_ref, cwk_ref, hg_ref, o_ref,
                  xp_ref, qs_ref, kt_ref, va_ref, a_ref, cum_ref, tot_ref, amx_ref, bcol_ref,
                  hf_ref, hb_ref, cf_ref, cb_ref):
    seq = q_ref.shape[1]
    nc = seq // CHUNK
    dh = A_HEAD_DIM
    halo = 8
    pad = (CONV_WIDTH - 1) // 2

    zeros_halo = jnp.zeros((halo, dh), F32)
    xp_ref[0:halo, :] = zeros_halo
    xp_ref[halo + seq:halo + seq + halo, :] = zeros_halo

    def conv_chunk(c, w):
        r0 = pl.multiple_of(c * CHUNK, CHUNK) + (halo - pad)
        y = xp_ref[pl.ds(r0, CHUNK), :] * w[0:1, :]
        for j in range(1, CONV_WIDTH):
            y = y + xp_ref[pl.ds(r0 + j, CHUNK), :] * w[j:j + 1, :]
        return y * _sigmoid(y)

    xp_ref[halo:halo + seq, :] = q_ref[0]
    wq = cwq_ref[...]

    def q_body(c, carry):
        r0 = pl.multiple_of(c * CHUNK, CHUNK)
        qs_ref[pl.ds(r0, CHUNK), :] = (conv_chunk(c, wq) * (dh ** -0.5)).astype(BF16)
        return carry

    lax.fori_loop(0, nc, q_body, 0, unroll=2)

    xp_ref[halo:halo + seq, :] = k_ref[0]
    wk = cwk_ref[...]

    def k_body(c, carry):
        r0 = pl.multiple_of(c * CHUNK, CHUNK)
        kt_ref[:, pl.ds(r0, CHUNK)] = conv_chunk(c, wk).T.astype(BF16)
        return carry

    lax.fori_loop(0, nc, k_body, 0, unroll=4)

    va_ref[:, 0:dh] = v_ref[0]
    va_ref[:, dh:2 * dh] = jnp.ones((seq, dh), BF16)

    a_ref[...] = jnp.zeros_like(a_ref)
    cum_ref[...] = jnp.zeros_like(cum_ref)
    head = pl.program_id(1)
    for kind, (table, d) in enumerate([(a_ref, 0), (cum_ref, 0), (a_ref, 1), (cum_ref, 1)]):
        gate = gt_ref[pl.ds(head * 4 + kind, 1), :]
        for c in range(nc):
            table[8 * c + d:8 * c + d + 1, :] = gate[:, c * CHUNK:(c + 1) * CHUNK]
    li = a_ref[...]
    fpre = cum_ref[...]
    lf = jnp.minimum(fpre, 0.0) - jnp.log1p(jnp.exp(-jnp.abs(fpre)))
    direction = lax.broadcasted_iota(jnp.int32, li.shape, 0) % 8
    ui = lax.broadcasted_iota(jnp.int32, (CHUNK, CHUNK), 0)
    si = lax.broadcasted_iota(jnp.int32, (CHUNK, CHUNK), 1)
    cum = (_split3_dot(jnp.where(direction == 0, lf, 0.0), jnp.where(ui <= si, 1.0, 0.0).astype(BF16))
           + _split3_dot(jnp.where(direction == 1, lf, 0.0), jnp.where(ui >= si, 1.0, 0.0).astype(BF16)))
    a_tab = li - cum
    total = jnp.where(direction == 0, cum[:, CHUNK - 1:CHUNK], cum[:, 0:1])
    a_ref[...] = a_tab
    tot_ref[...] = jnp.broadcast_to(total, a_tab.shape)
    amx_ref[...] = jnp.broadcast_to(jnp.max(a_tab, axis=-1, keepdims=True), a_tab.shape)

    cum_ref[...] = cum

    def bcol_body(c, carry):
        rs = pl.ds(pl.multiple_of(c * CHUNK, CHUNK), CHUNK)
        tile = cum_ref[pl.ds(pl.multiple_of(c * 8, 8), 8), :]
        for d in range(2):
            bcol_ref[d, rs, :] = jnp.broadcast_to(tile[d:d + 1, :], (CHUNK, CHUNK)).T
        return carry

    lax.fori_loop(0, nc, bcol_body, 0, unroll=4)

    cf_ref[...] = jnp.zeros_like(cf_ref)
    cb_ref[...] = jnp.zeros_like(cb_ref)

    def step(c, c_ref, m, d):
        r0 = pl.multiple_of(c * CHUNK, CHUNK)
        g0 = pl.multiple_of(c * 8, 8)
        qc = qs_ref[pl.ds(r0, CHUNK), :]
        ktc = kt_ref[:, pl.ds(r0, CHUNK)]
        vc = va_ref[pl.ds(r0, CHUNK), :]
        a = a_ref[pl.ds(g0, 8), :][d:d + 1, :]
        tot = tot_ref[pl.ds(g0, 8), :][d:d + 1, :]
        amax = amx_ref[pl.ds(g0, 8), :][d:d + 1, :]
        visible = (ui >= si) if d == 0 else (ui <= si)
        am = jnp.where(visible, a, -jnp.inf)
        reach = jnp.broadcast_to(jnp.max(am, axis=-1, keepdims=True), (CHUNK, CHUNK))
        m_row = jnp.maximum(m, reach)
        inter = jnp.exp(m - m_row)
        s = _dot(qc, ktc) * jnp.exp(am - m_row)
        m_new = tot + jnp.maximum(m, amax)
        decay = jnp.exp(tot + m - m_new)
        kw = (ktc.astype(F32) * jnp.exp(tot + a - m_new)).astype(BF16)
        sv = _dot(jnp.concatenate([s.astype(BF16), kw], axis=0), vc)
        nd = sv[0:CHUNK, :] + jnp.concatenate([inter, inter], axis=1) * _dot(qc, c_ref[...].astype(BF16))
        clamp = jnp.exp(-(bcol_ref[d, pl.ds(r0, CHUNK), :] + m_row))
        h = nd[:, 0:dh] / jnp.maximum(jnp.abs(nd[:, dh:2 * dh]), clamp)
        c_ref[...] = jnp.concatenate([decay, decay], axis=1) * c_ref[...] + sv[CHUNK:2 * CHUNK, :]
        return h, m_new

    def scan_body(c, carry):
        m_f, m_b = carry
        h_f, m_f = step(c, cf_ref, m_f, 0)
        hf_ref[pl.ds(pl.multiple_of(c * CHUNK, CHUNK), CHUNK), :] = h_f
        cb = nc - 1 - c
        h_b, m_b = step(cb, cb_ref, m_b, 1)
        hb_ref[pl.ds(pl.multiple_of(cb * CHUNK, CHUNK), CHUNK), :] = h_b
        return m_f, m_b

    m_init = jnp.zeros((1, CHUNK), F32)
    lax.fori_loop(0, nc, scan_body, (m_init, m_init), unroll=8)

    def out_body(c, carry):
        rs = pl.ds(pl.multiple_of(c * CHUNK, CHUNK), CHUNK)
        y = _rms(hf_ref[rs, :] + hb_ref[rs, :], hg_ref[...])
        o_ref[0, rs, :] = (y * _sigmoid(oa_ref[0, rs, :])).astype(BF16)
        return carry

    lax.fori_loop(0, nc, out_body, 0, unroll=4)


def _mlstm(qk, va, oa, gt, conv_w, head_g):
    bsz, seq, _ = qk.shape
    nc = seq // CHUNK
    dh = A_HEAD_DIM
    col = lambda off: pl.BlockSpec((1, seq, dh), lambda b, h: (b, 0, h + off))
    table = pltpu.VMEM((8 * nc, CHUNK), F32)
    return pl.pallas_call(
        _mlstm_kernel,
        grid=(bsz, A_HEADS),
        in_specs=[col(0), col(A_HEADS), col(0), col(0),
                  pl.BlockSpec((4 * A_HEADS, seq), lambda b, h: (0, b)),
                  pl.BlockSpec((CONV_WIDTH, dh), lambda b, h: (0, h)),
                  pl.BlockSpec((CONV_WIDTH, dh), lambda b, h: (0, h + A_HEADS)),
                  pl.BlockSpec((1, dh), lambda b, h: (0, h))],
        out_specs=col(0),
        out_shape=jax.ShapeDtypeStruct((bsz, seq, A_WIDTH), BF16),
        scratch_shapes=[pltpu.VMEM((seq + 16, dh), F32),
                        pltpu.VMEM((seq, dh), BF16),
                        pltpu.VMEM((dh, seq), BF16),
                        pltpu.VMEM((seq, 2 * dh), BF16),
                        table,
                        table,
                        table,
                        table,
                        pltpu.VMEM((2, seq, CHUNK), F32),
                        pltpu.VMEM((seq, dh), F32),
                        pltpu.VMEM((seq, dh), F32),
                        pltpu.VMEM((dh, 2 * dh), F32),
                        pltpu.VMEM((dh, 2 * dh), F32)],
        compiler_params=_params(2),
        name="mlstm",
    )(qk, qk, va, oa, gt, conv_w, conv_w, head_g)


def _rope_table_kernel(pos_ref, freq_ref, ct_ref, st_ref):
    half = C_ROPE // 2
    hp = HEAD_PAD
    ang = pos_ref[...].astype(F32) * freq_ref[...]
    cs = jnp.cos(ang)
    sn = jnp.sin(ang)
    ct_ref[...] = jnp.ones_like(ct_ref)
    st_ref[...] = jnp.zeros_like(st_ref)
    ct_ref[0:half, :] = cs
    ct_ref[hp // 2:hp // 2 + half, :] = cs
    st_ref[0:half, :] = -sn
    st_ref[hp // 2:hp // 2 + half, :] = sn


def _rope_tables(pos_row, freq_col, tn=2048):
    n = pos_row.shape[1]
    tn = min(tn, n)
    out_spec = pl.BlockSpec((HEAD_PAD, tn), lambda i: (0, i))
    out = jax.ShapeDtypeStruct((HEAD_PAD, n), F32)
    return pl.pallas_call(
        _rope_table_kernel,
        grid=(n // tn,),
        in_specs=[pl.BlockSpec((1, tn), lambda i: (0, i)), _const_spec(freq_col.shape)],
        out_specs=[out_spec, out_spec],
        out_shape=[out, out],
        compiler_params=_params(1),
        name="rope_tables",
    )(pos_row, freq_col)


def _mla_in_kernel(x_ref, g_ref, w1_ref, qg_ref, wq_ref, kvg_ref, wk_ref, wv_ref, ct_ref, st_ref,
                   q_ref, k_ref, v_ref):
    hp = HEAD_PAD
    h = _rms(x_ref[...], g_ref[...]).astype(BF16)
    p1 = _dot(h, w1_ref[...])
    ct = ct_ref[...]
    st = st_ref[...]

    def rotary(a):
        partner = jnp.concatenate([a[hp // 2:, :], a[:hp // 2, :]], axis=0)
        return a * ct + partner * st

    k_rope = rotary(p1[:, Q_LORA + KV_LORA:].T).T

    cq = _rms(p1[:, :Q_LORA], qg_ref[...]).astype(BF16)
    qa = lax.dot_general(wq_ref[...], cq, (((1,), (1,)), ((), ())), preferred_element_type=F32)
    scale = (C_NOPE + C_ROPE) ** -0.5 * LOG2E
    for hd in range(C_HEADS):
        q_ref[hd * hp:(hd + 1) * hp, :] = (rotary(qa[hd * hp:(hd + 1) * hp, :]) * scale).astype(BF16)

    ckv = _rms(p1[:, Q_LORA:Q_LORA + KV_LORA], kvg_ref[...]).astype(BF16)
    kn = _dot(ckv, wk_ref[...])
    for hd in range(C_HEADS):
        k_ref[:, hd * hp:(hd + 1) * hp] = (kn[:, hd * hp:(hd + 1) * hp] + k_rope).astype(BF16)
    v_ref[...] = lax.dot_general(wv_ref[...], ckv, (((1,), (1,)), ((), ())),
                                 preferred_element_type=F32).astype(BF16)


def _mla_in(x2, g, w1, qg, wab, kvg, wk, wv, ct, st, tm=1024):
    n = x2.shape[0]
    row = lambda w: pl.BlockSpec((tm, w), lambda i: (i, 0))
    col = lambda h: pl.BlockSpec((h, tm), lambda i: (0, i))
    return pl.pallas_call(
        _mla_in_kernel,
        grid=(n // tm,),
        in_specs=[row(D_MODEL), _const_spec(g.shape), _const_spec(w1.shape), _const_spec(qg.shape),
                  _const_spec(wab.shape), _const_spec(kvg.shape), _const_spec(wk.shape),
                  _const_spec(wv.shape), col(HEAD_PAD), col(HEAD_PAD)],
        out_specs=[col(C_HEADS * HEAD_PAD), row(C_HEADS * HEAD_PAD), col(C_HEADS * C_V)],
        out_shape=[jax.ShapeDtypeStruct((C_HEADS * HEAD_PAD, n), BF16),
                   jax.ShapeDtypeStruct((n, C_HEADS * HEAD_PAD), BF16),
                   jax.ShapeDtypeStruct((C_HEADS * C_V, n), BF16)],
        compiler_params=_params(1),
        name="mla_in",
    )(x2, g, w1, qg, wab, kvg, wk, wv, ct, st)


def _mla_attn_kernel(qt_ref, k_ref, vt_ref, o_ref, vx_ref, sa_ref, sb_ref, *, tq):
    seq = k_ref.shape[1]
    hp = HEAD_PAD
    n_tiles = seq // tq
    n_blocks = seq // KEY_BLOCK
    n_items = ATTN_PAIRS * n_tiles
    ones_rows = V_EXT_ROWS - C_V
    for hd in range(2 * ATTN_PAIRS):
        vx_ref[hd, 0:C_V, :] = vt_ref[hd * C_V:(hd + 1) * C_V, :]
        vx_ref[hd, C_V:V_EXT_ROWS, :] = jnp.ones((ones_rows, seq), BF16)

    def keys(j):
        return slice(j * KEY_BLOCK, (j + 1) * KEY_BLOCK)

    def stage(score_item, score_ref, value_item, value_ref, m_prev):
        maxima = [None, None]
        accs = [None, None]
        if score_ref is not None:
            sp = score_item // n_tiles
            srows = pl.ds(pl.multiple_of((score_item % n_tiles) * tq, tq), tq)
            slanes = [pl.ds(pl.multiple_of((2 * sp + hh) * hp, hp), hp) for hh in range(2)]
            qts = [qt_ref[slanes[hh], srows] for hh in range(2)]
        if value_ref is not None:
            vp = value_item // n_tiles
            vrows = pl.ds(pl.multiple_of((value_item % n_tiles) * tq, tq), tq)
        for j in range(n_blocks):
            for hh in range(2):
                if score_ref is not None:
                    st = _dot(k_ref[0, keys(j), slanes[hh]], qts[hh])
                    score_ref[hh, keys(j), :] = st
                    m = jnp.max(st, axis=0, keepdims=True)
                    maxima[hh] = m if j == 0 else jnp.maximum(maxima[hh], m)
                if value_ref is not None:
                    pt = jnp.exp2(value_ref[hh, keys(j), :] - m_prev[hh]).astype(BF16)
                    o = _dot(vx_ref[2 * vp + hh, :, keys(j)], pt)
                    accs[hh] = o if j == 0 else accs[hh] + o
        if value_ref is not None:
            outs = [acc[0:C_V, :] / acc[C_V:C_V + 1, :] for acc in accs]
            olanes = pl.ds(pl.multiple_of(vp * 2 * C_V, 2 * C_V), 2 * C_V)
            o_ref[0, vrows, olanes] = jnp.concatenate(outs, axis=0).T.astype(BF16)
        return tuple(maxima)

    def pair_body(i, m_prev):
        m_mid = stage(2 * i + 1, sb_ref, 2 * i, sa_ref, m_prev)
        return stage(2 * i + 2, sa_ref, 2 * i + 1, sb_ref, m_mid)

    m_even = lax.fori_loop(0, n_items // 2 - 1, pair_body, stage(0, sa_ref, None, None, None))
    m_odd = stage(n_items - 1, sb_ref, n_items - 2, sa_ref, m_even)
    stage(None, None, n_items - 1, sb_ref, m_odd)


def _mla_attn(qt, k, vt, tq=256):
    bsz, seq, _ = k.shape
    heads = 2 * ATTN_PAIRS
    scores = pltpu.VMEM((2, seq, tq), F32)
    return pl.pallas_call(
        functools.partial(_mla_attn_kernel, tq=tq),
        grid=(bsz, C_HEADS // heads),
        in_specs=[pl.BlockSpec((heads * HEAD_PAD, seq), lambda b, h: (h, b)),
                  pl.BlockSpec((1, seq, heads * HEAD_PAD), lambda b, h: (b, 0, h)),
                  pl.BlockSpec((heads * C_V, seq), lambda b, h: (h, b))],
        out_specs=pl.BlockSpec((1, seq, heads * C_V), lambda b, h: (b, 0, h)),
        out_shape=jax.ShapeDtypeStruct((bsz, seq, C_HEADS * C_V), BF16),
        scratch_shapes=[pltpu.VMEM((heads, V_EXT_ROWS, seq), BF16),
                        scores, scores],
        compiler_params=_params(2),
        name="mla_attn",
    )(qt, k, vt)


def _mix_ffn_kernel(*refs, n_mix, tf, final):
    x_ref = refs[0]
    mix_refs = refs[1:1 + n_mix]
    wmix_ref, g_ref, w1_ref, w2_ref = refs[1 + n_mix:5 + n_mix]
    fg_ref = refs[5 + n_mix] if final else None
    o_ref = refs[-1]

    mix = jnp.concatenate([a_ref[...] for a_ref in mix_refs], axis=1)
    x = x_ref[...] + _dot(mix, wmix_ref[...])
    h = _rms(x, g_ref[...]).astype(BF16)
    for j in range(D_FF // tf):
        a = _dot(h, w1_ref[:, j * tf:(j + 1) * tf])
        a = jnp.square(jnp.maximum(a, 0.0)).astype(BF16)
        x = x + _dot(a, w2_ref[j * tf:(j + 1) * tf, :])
    if final:
        x = _rms(x, fg_ref[...])
    o_ref[...] = x


def _mix_ffn(x2, mixes, wmix, g, w1, w2, final_g=None, tm=1024, tf=512):
    n = x2.shape[0]
    row = lambda w: pl.BlockSpec((tm, w), lambda i: (i, 0))
    final = final_g is not None
    args = [x2, *mixes, wmix, g, w1, w2] + ([final_g] if final else [])
    in_specs = ([row(D_MODEL)] + [row(a.shape[1]) for a in mixes]
                + [_const_spec(a.shape) for a in args[1 + len(mixes):]])
    return pl.pallas_call(
        functools.partial(_mix_ffn_kernel, n_mix=len(mixes), tf=tf, final=final),
        grid=(n // tm,),
        in_specs=in_specs,
        out_specs=row(D_MODEL),
        out_shape=jax.ShapeDtypeStruct((n, D_MODEL), F32),
        compiler_params=_params(1),
        name="mix_ffn",
    )(*args)


def _pad_heads(w, width, parts):
    k = w.shape[0]
    w = w.reshape(k, C_HEADS, width)
    cols = [w[:, :, a:b] if a is not None else jnp.zeros((k, C_HEADS, b), w.dtype) for a, b in parts]
    used = sum(c.shape[2] for c in cols)
    cols.append(jnp.zeros((k, C_HEADS, HEAD_PAD - used), w.dtype))
    return jnp.concatenate(cols, axis=2).reshape(k, C_HEADS * HEAD_PAD)


def kernel(x, positions, ab_norm, ab_w_in, ab_conv, ab_gate_b, ab_head_g, ab_v_g, ab_ws, ab_bs,
           ab_w_out, c_norm, c_w_in, c_q_g, c_kv_g, c_w_uq, c_w_ukv, c_w_out, ffn_norm, ffn_w1,
           ffn_w2, final_norm):
    bsz, seq, _ = x.shape
    n = bsz * seq
    nc = seq // CHUNK
    half = C_ROPE // 2
    x2 = x.reshape(n, D_MODEL)

    freq = ROPE_BASE ** (-jnp.arange(half, dtype=F32) / half)
    gap = HEAD_PAD // 2 - half
    cos_t, sin_t = _rope_tables(positions.reshape(1, n), freq.reshape(half, 1))

    for layer in range(DEPTH):
        j = layer // 2
        final_g = final_norm.reshape(1, D_MODEL) if layer == DEPTH - 1 else None
        ffn_args = (ffn_norm[layer].reshape(1, D_MODEL), ffn_w1[layer].astype(BF16),
                    ffn_w2[layer].astype(BF16))
        if layer % 2 == 0:
            w = ab_w_in[j].astype(BF16)
            o0 = 4 * A_WIDTH
            o1 = o0 + 4 * A_HEADS
            wg = w[:, o0:o1].reshape(D_MODEL, 4, A_HEADS).transpose(0, 2, 1).reshape(D_MODEL, 4 * A_HEADS)
            wg = jnp.pad(wg, ((0, 0), (0, LANES - 4 * A_HEADS)))
            gb = jnp.pad(ab_gate_b[j].reshape(4, A_HEADS).T.reshape(1, 4 * A_HEADS),
                         ((0, 0), (0, LANES - 4 * A_HEADS)))
            qk, va, oa, gates_t, hb = _ab_in(
                x2, ab_norm[j].reshape(1, D_MODEL), w[:, :2 * A_WIDTH], w[:, 2 * A_WIDTH:3 * A_WIDTH],
                w[:, 3 * A_WIDTH:o0], wg, gb, w[:, o1:], ab_v_g[j].reshape(1, B_WIDTH),
                ab_ws[j].astype(BF16), ab_bs[j].T)
            ha = _mlstm(qk.reshape(bsz, seq, 2 * A_WIDTH), va.reshape(bsz, seq, A_WIDTH),
                        oa.reshape(bsz, seq, A_WIDTH), gates_t, ab_conv[j],
                        ab_head_g[j].reshape(1, A_WIDTH))
            x2 = _mix_ffn(x2, [ha.reshape(n, A_WIDTH), hb], ab_w_out[j].astype(BF16),
                          *ffn_args, final_g=final_g)
        else:
            w_in = c_w_in[j]
            kr1 = w_in[:, Q_LORA + KV_LORA:Q_LORA + KV_LORA + half]
            kr2 = w_in[:, Q_LORA + KV_LORA + half:]
            zc = lambda w: jnp.zeros((D_MODEL, w), w_in.dtype)
            w1 = jnp.concatenate([w_in[:, :Q_LORA + KV_LORA], kr1, zc(gap), kr2, zc(gap)],
                                 axis=1).astype(BF16)
            qw = C_NOPE + C_ROPE
            wab = _pad_heads(c_w_uq[j], qw, [(C_NOPE, C_NOPE + half), (0, gap), (C_NOPE + half, qw),
                                             (gap, C_NOPE)]).T.astype(BF16)
            wk = _pad_heads(c_w_ukv[j], C_NOPE + C_V, [(None, half), (0, gap), (None, half),
                                                       (gap, C_NOPE)]).astype(BF16)
            wv = c_w_ukv[j].reshape(KV_LORA, C_HEADS, C_NOPE + C_V)[:, :, C_NOPE:]
            wv_t = wv.reshape(KV_LORA, C_HEADS * C_V).T.astype(BF16)
            qt, k, vt = _mla_in(x2, c_norm[j].reshape(1, D_MODEL), w1, c_q_g[j].reshape(1, Q_LORA), wab,
                                c_kv_g[j].reshape(1, KV_LORA), wk, wv_t, cos_t, sin_t)
            o = _mla_attn(qt, k.reshape(bsz, seq, -1), vt)
            x2 = _mix_ffn(x2, [o.reshape(n, C_HEADS * C_V)], c_w_out[j].astype(BF16),
                          *ffn_args, final_g=final_g)
    return x2.reshape(bsz, seq, D_MODEL)
```

```python
import functools

import jax
import jax.numpy as jnp
from jax import lax
from jax.experimental import pallas as pl
from jax.experimental.pallas import tpu as pltpu

F32 = jnp.float32
BF16 = jnp.bfloat16

D_MODEL = 1024
DEPTH = 4
A_WIDTH = 512
A_HEADS = 4
A_HEAD_DIM = 128
CHUNK = 128
CONV_WIDTH = 5
B_WIDTH = 512
B_GROUPS = 4
B_GROUP_DIM = 128
C_HEADS = 16
C_NOPE = 64
C_ROPE = 32
C_V = 64
Q_LORA = 384
KV_LORA = 256
ROPE_BASE = 10000.0
D_FF = 4 * D_MODEL
EPS = 1e-6

LANES = 128
HEAD_PAD = 128
V_EXT_ROWS = C_V + 16
LOG2E = 1.4426950408889634
KEY_BLOCK = 256
ATTN_PAIRS = 2
VMEM_LIMIT = 56 * 1024 * 1024


def _params(n_axes):
    return pltpu.CompilerParams(
        dimension_semantics=("arbitrary",) * n_axes, vmem_limit_bytes=VMEM_LIMIT)


def _const_spec(shape):
    nd = len(shape)
    return pl.BlockSpec(shape, lambda *_: (0,) * nd, pipeline_mode=pl.Buffered(1))


def _rms(x, g):
    return x * lax.rsqrt(jnp.mean(x * x, axis=-1, keepdims=True) + EPS) * g


def _dot(a, b):
    return jnp.dot(a, b, preferred_element_type=F32)


def _sigmoid(x):
    return 1.0 / (1.0 + jnp.exp(-x))


def _ab_in_kernel(x_ref, g_ref, wa_ref, wg_ref, gb_ref, wuv_ref, vg_ref, ws_ref, bs_ref,
                  qk_ref, va_ref, oa_ref, gt_ref, hb_ref):
    tm = x_ref.shape[0]
    h = _rms(x_ref[...], g_ref[...]).astype(BF16)
    qk_ref[...] = _dot(h, wa_ref[:, 0:2 * A_WIDTH])
    va_ref[...] = _dot(h, wa_ref[:, 2 * A_WIDTH:3 * A_WIDTH]).astype(BF16)
    oa_ref[...] = _dot(h, wa_ref[:, 3 * A_WIDTH:4 * A_WIDTH])
    gt_ref[...] = (_dot(h, wg_ref[...]) + gb_ref[...]).T[0:4 * A_HEADS, :]

    uv = _dot(h, wuv_ref[...])
    uv = 0.5 * uv * (1.0 + jnp.tanh(0.7978845608028654 * (uv + 0.044715 * (uv * uv * uv))))
    n_chunks = tm // CHUNK
    for g in range(B_GROUPS):
        lo = g * B_GROUP_DIM
        u = uv[:, lo:lo + B_GROUP_DIM]
        vb = uv[:, B_WIDTH + lo:B_WIDTH + lo + B_GROUP_DIM]
        vb = _rms(vb, vg_ref[:, lo:lo + B_GROUP_DIM]).astype(BF16)
        rhs = jnp.concatenate([vb[c * CHUNK:(c + 1) * CHUNK, :] for c in range(n_chunks)], axis=1)
        sp = _dot(ws_ref[g], rhs) + bs_ref[:, g:g + 1]
        for c in range(n_chunks):
            hb_ref[c * CHUNK:(c + 1) * CHUNK, lo:lo + B_GROUP_DIM] = (
                u[c * CHUNK:(c + 1) * CHUNK, :] * sp[:, c * B_GROUP_DIM:(c + 1) * B_GROUP_DIM]
            ).astype(BF16)


def _ab_in(x2, g, wa, wg, gb, wuv, vg, ws, bs_t, tm=1024):
    n = x2.shape[0]
    row = lambda w: pl.BlockSpec((tm, w), lambda i: (i, 0))
    return pl.pallas_call(
        _ab_in_kernel,
        grid=(n // tm,),
        in_specs=[row(D_MODEL), _const_spec(g.shape), _const_spec(wa.shape), _const_spec(wg.shape),
                  _const_spec(gb.shape), _const_spec(wuv.shape), _const_spec(vg.shape),
                  _const_spec(ws.shape), _const_spec(bs_t.shape)],
        out_specs=[row(2 * A_WIDTH), row(A_WIDTH), row(A_WIDTH),
                   pl.BlockSpec((4 * A_HEADS, tm), lambda i: (0, i)), row(B_WIDTH)],
        out_shape=[jax.ShapeDtypeStruct((n, 2 * A_WIDTH), F32),
                   jax.ShapeDtypeStruct((n, A_WIDTH), BF16),
                   jax.ShapeDtypeStruct((n, A_WIDTH), F32),
                   jax.ShapeDtypeStruct((4 * A_HEADS, n), F32),
                   jax.ShapeDtypeStruct((n, B_WIDTH), BF16)],
        compiler_params=_params(1),
        name="ab_in",
    )(x2, g, wa, wg, gb, wuv, vg, ws, bs_t)


def _split3_dot(x, tri):
    x0 = x.astype(BF16)
    r1 = x - x0.astype(F32)
    x1 = r1.astype(BF16)
    x2 = (r1 - x1.astype(F32)).astype(BF16)
    return _dot(x0, tri) + _dot(x1, tri) + _dot(x2, tri)


def _mlstm_kernel(q_ref, k_ref, v_ref, oa_ref, gt_ref, cwq_ref, cwk_ref, hg_ref, o_ref,
                  xp_ref, qs_ref, kt_ref, va_ref, a_ref, cum_ref, tot_ref, amx_ref, bcol_ref,
                  hf_ref, hb_ref, cf_ref, cb_ref):
    seq = q_ref.shape[1]
    nc = seq // CHUNK
    dh = A_HEAD_DIM
    halo = 8
    pad = (CONV_WIDTH - 1) // 2

    zeros_halo = jnp.zeros((halo, dh), F32)
    xp_ref[0:halo, :] = zeros_halo
    xp_ref[halo + seq:halo + seq + halo, :] = zeros_halo

    def conv_chunk(c, w):
        r0 = pl.multiple_of(c * CHUNK, CHUNK) + (halo - pad)
        y = xp_ref[pl.ds(r0, CHUNK), :] * w[0:1, :]
        for j in range(1, CONV_WIDTH):
            y = y + xp_ref[pl.ds(r0 + j, CHUNK), :] * w[j:j + 1, :]
        return y * _sigmoid(y)

    xp_ref[halo:halo + seq, :] = q_ref[0]
    wq = cwq_ref[...]

    def q_body(c, carry):
        r0 = pl.multiple_of(c * CHUNK, CHUNK)
        qs_ref[pl.ds(r0, CHUNK), :] = (conv_chunk(c, wq) * (dh ** -0.5)).astype(BF16)
        return carry

    lax.fori_loop(0, nc, q_body, 0, unroll=2)

    xp_ref[halo:halo + seq, :] = k_ref[0]
    wk = cwk_ref[...]

    def k_body(c, carry):
        r0 = pl.multiple_of(c * CHUNK, CHUNK)
        kt_ref[:, pl.ds(r0, CHUNK)] = conv_chunk(c, wk).T.astype(BF16)
        return carry

    lax.fori_loop(0, nc, k_body, 0, unroll=4)

    va_ref[:, 0:dh] = v_ref[0]
    va_ref[:, dh:2 * dh] = jnp.ones((seq, dh), BF16)

    a_ref[...] = jnp.zeros_like(a_ref)
    cum_ref[...] = jnp.zeros_like(cum_ref)
    head = pl.program_id(1)
    for kind, (table, d) in enumerate([(a_ref, 0), (cum_ref, 0), (a_ref, 1), (cum_ref, 1)]):
        gate = gt_ref[pl.ds(head * 4 + kind, 1), :]
        for c in range(nc):
            table[8 * c + d:8 * c + d + 1, :] = gate[:, c * CHUNK:(c + 1) * CHUNK]
    li = a_ref[...]
    fpre = cum_ref[...]
    lf = jnp.minimum(fpre, 0.0) - jnp.log1p(jnp.exp(-jnp.abs(fpre)))
    direction = lax.broadcasted_iota(jnp.int32, li.shape, 0) % 8
    ui = lax.broadcasted_iota(jnp.int32, (CHUNK, CHUNK), 0)
    si = lax.broadcasted_iota(jnp.int32, (CHUNK, CHUNK), 1)
    cum = (_split3_dot(jnp.where(direction == 0, lf, 0.0), jnp.where(ui <= si, 1.0, 0.0).astype(BF16))
           + _split3_dot(jnp.where(direction == 1, lf, 0.0), jnp.where(ui >= si, 1.0, 0.0).astype(BF16)))
    a_tab = li - cum
    total = jnp.where(direction == 0, cum[:, CHUNK - 1:CHUNK], cum[:, 0:1])
    a_ref[...] = a_tab
    tot_ref[...] = jnp.broadcast_to(total, a_tab.shape)
    amx_ref[...] = jnp.broadcast_to(jnp.max(a_tab, axis=-1, keepdims=True), a_tab.shape)

    cum_ref[...] = cum

    def bcol_body(c, carry):
        rs = pl.ds(pl.multiple_of(c * CHUNK, CHUNK), CHUNK)
        tile = cum_ref[pl.ds(pl.multiple_of(c * 8, 8), 8), :]
        for d in range(2):
            bcol_ref[d, rs, :] = jnp.broadcast_to(tile[d:d + 1, :], (CHUNK, CHUNK)).T
        return carry

    lax.fori_loop(0, nc, bcol_body, 0, unroll=4)

    cf_ref[...] = jnp.zeros_like(cf_ref)
    cb_ref[...] = jnp.zeros_like(cb_ref)

    def step(c, c_ref, m, d):
        r0 = pl.multiple_of(c * CHUNK, CHUNK)
        g0 = pl.multiple_of(c * 8, 8)
        qc = qs_ref[pl.ds(r0, CHUNK), :]
        ktc = kt_ref[:, pl.ds(r0, CHUNK)]
        vc = va_ref[pl.ds(r0, CHUNK), :]
        a = a_ref[pl.ds(g0, 8), :][d:d + 1, :]
        tot = tot_ref[pl.ds(g0, 8), :][d:d + 1, :]
        amax = amx_ref[pl.ds(g0, 8), :][d:d + 1, :]
        visible = (ui >= si) if d == 0 else (ui <= si)
        am = jnp.where(visible, a, -jnp.inf)
        reach = jnp.broadcast_to(jnp.max(am, axis=-1, keepdims=True), (CHUNK, CHUNK))
        m_row = jnp.maximum(m, reach)
        inter = jnp.exp(m - m_row)
        s = _dot(qc, ktc) * jnp.exp(am - m_row)
        m_new = tot + jnp.maximum(m, amax)
        decay = jnp.exp(tot + m - m_new)
        kw = (ktc.astype(F32) * jnp.exp(tot + a - m_new)).astype(BF16)
        iq = (qc.astype(F32) * inter).astype(BF16)
        lhs = jnp.concatenate([jnp.concatenate([s.astype(BF16), iq], axis=1),
                               jnp.concatenate([kw, jnp.zeros_like(kw)], axis=1)], axis=0)
        nd_u = _dot(lhs, jnp.concatenate([vc, c_ref[...].astype(BF16)], axis=0))
        clamp = jnp.exp(-(bcol_ref[d, pl.ds(r0, CHUNK), :] + m_row))
        h = nd_u[0:CHUNK, 0:dh] / jnp.maximum(jnp.abs(nd_u[0:CHUNK, dh:2 * dh]), clamp)
        c_ref[...] = jnp.concatenate([decay, decay], axis=1) * c_ref[...] + nd_u[CHUNK:2 * CHUNK, :]
        return h, m_new

    def scan_body(c, carry):
        m_f, m_b = carry
        h_f, m_f = step(c, cf_ref, m_f, 0)
        hf_ref[pl.ds(pl.multiple_of(c * CHUNK, CHUNK), CHUNK), :] = h_f
        cb = nc - 1 - c
        h_b, m_b = step(cb, cb_ref, m_b, 1)
        hb_ref[pl.ds(pl.multiple_of(cb * CHUNK, CHUNK), CHUNK), :] = h_b
        return m_f, m_b

    m_init = jnp.zeros((1, CHUNK), F32)
    lax.fori_loop(0, nc, scan_body, (m_init, m_init), unroll=8)

    def out_body(c, carry):
        rs = pl.ds(pl.multiple_of(c * CHUNK, CHUNK), CHUNK)
        y = _rms(hf_ref[rs, :] + hb_ref[rs, :], hg_ref[...])
        o_ref[0, rs, :] = (y * _sigmoid(oa_ref[0, rs, :])).astype(BF16)
        return carry

    lax.fori_loop(0, nc, out_body, 0, unroll=4)


def _mlstm(qk, va, oa, gt, conv_w, head_g):
    bsz, seq, _ = qk.shape
    nc = seq // CHUNK
    dh = A_HEAD_DIM
    col = lambda off: pl.BlockSpec((1, seq, dh), lambda b, h: (b, 0, h + off))
    table = pltpu.VMEM((8 * nc, CHUNK), F32)
    return pl.pallas_call(
        _mlstm_kernel,
        grid=(bsz, A_HEADS),
        in_specs=[col(0), col(A_HEADS), col(0), col(0),
                  pl.BlockSpec((4 * A_HEADS, seq), lambda b, h: (0, b)),
                  pl.BlockSpec((CONV_WIDTH, dh), lambda b, h: (0, h)),
                  pl.BlockSpec((CONV_WIDTH, dh), lambda b, h: (0, h + A_HEADS)),
                  pl.BlockSpec((1, dh), lambda b, h: (0, h))],
        out_specs=col(0),
        out_shape=jax.ShapeDtypeStruct((bsz, seq, A_WIDTH), BF16),
        scratch_shapes=[pltpu.VMEM((seq + 16, dh), F32),
                        pltpu.VMEM((seq, dh), BF16),
                        pltpu.VMEM((dh, seq), BF16),
                        pltpu.VMEM((seq, 2 * dh), BF16),
                        table,
                        table,
                        table,
                        table,
                        pltpu.VMEM((2, seq, CHUNK), F32),
                        pltpu.VMEM((seq, dh), F32),
                        pltpu.VMEM((seq, dh), F32),
                        pltpu.VMEM((dh, 2 * dh), F32),
                        pltpu.VMEM((dh, 2 * dh), F32)],
        compiler_params=_params(2),
        name="mlstm",
    )(qk, qk, va, oa, gt, conv_w, conv_w, head_g)


def _rope_table_kernel(pos_ref, freq_ref, ct_ref, st_ref):
    half = C_ROPE // 2
    hp = HEAD_PAD
    ang = pos_ref[...].astype(F32) * freq_ref[...]
    cs = jnp.cos(ang)
    sn = jnp.sin(ang)
    ct_ref[...] = jnp.ones_like(ct_ref)
    st_ref[...] = jnp.zeros_like(st_ref)
    ct_ref[0:half, :] = cs
    ct_ref[hp // 2:hp // 2 + half, :] = cs
    st_ref[0:half, :] = -sn
    st_ref[hp // 2:hp // 2 + half, :] = sn


def _rope_tables(pos_row, freq_col, tn=2048):
    n = pos_row.shape[1]
    tn = min(tn, n)
    out_spec = pl.BlockSpec((HEAD_PAD, tn), lambda i: (0, i))
    out = jax.ShapeDtypeStruct((HEAD_PAD, n), F32)
    return pl.pallas_call(
        _rope_table_kernel,
        grid=(n // tn,),
        in_specs=[pl.BlockSpec((1, tn), lambda i: (0, i)), _const_spec(freq_col.shape)],
        out_specs=[out_spec, out_spec],
        out_shape=[out, out],
        compiler_params=_params(1),
        name="rope_tables",
    )(pos_row, freq_col)


def _mla_in_kernel(x_ref, g_ref, w1_ref, qg_ref, wq_ref, kvg_ref, wk_ref, wv_ref, ct_ref, st_ref,
                   q_ref, k_ref, v_ref):
    hp = HEAD_PAD
    h = _rms(x_ref[...], g_ref[...]).astype(BF16)
    p1 = _dot(h, w1_ref[...])
    ct = ct_ref[...]
    st = st_ref[...]

    def rotary(a):
        partner = jnp.concatenate([a[hp // 2:, :], a[:hp // 2, :]], axis=0)
        return a * ct + partner * st

    k_rope = rotary(p1[:, Q_LORA + KV_LORA:].T).T

    cq = _rms(p1[:, :Q_LORA], qg_ref[...]).astype(BF16)
    qa = lax.dot_general(wq_ref[...], cq, (((1,), (1,)), ((), ())), preferred_element_type=F32)
    scale = (C_NOPE + C_ROPE) ** -0.5 * LOG2E
    for hd in range(C_HEADS):
        q_ref[hd * hp:(hd + 1) * hp, :] = (rotary(qa[hd * hp:(hd + 1) * hp, :]) * scale).astype(BF16)

    ckv = _rms(p1[:, Q_LORA:Q_LORA + KV_LORA], kvg_ref[...]).astype(BF16)
    kn = _dot(ckv, wk_ref[...])
    for hd in range(C_HEADS):
        k_ref[:, hd * hp:(hd + 1) * hp] = (kn[:, hd * hp:(hd + 1) * hp] + k_rope).astype(BF16)
    v_ref[...] = lax.dot_general(wv_ref[...], ckv, (((1,), (1,)), ((), ())),
                                 preferred_element_type=F32).astype(BF16)


def _mla_in(x2, g, w1, qg, wab, kvg, wk, wv, ct, st, tm=1024):
    n = x2.shape[0]
    row = lambda w: pl.BlockSpec((tm, w), lambda i: (i, 0))
    col = lambda h: pl.BlockSpec((h, tm), lambda i: (0, i))
    return pl.pallas_call(
        _mla_in_kernel,
        grid=(n // tm,),
        in_specs=[row(D_MODEL), _const_spec(g.shape), _const_spec(w1.shape), _const_spec(qg.shape),
                  _const_spec(wab.shape), _const_spec(kvg.shape), _const_spec(wk.shape),
                  _const_spec(wv.shape), col(HEAD_PAD), col(HEAD_PAD)],
        out_specs=[col(C_HEADS * HEAD_PAD), row(C_HEADS * HEAD_PAD), col(C_HEADS * C_V)],
        out_shape=[jax.ShapeDtypeStruct((C_HEADS * HEAD_PAD, n), BF16),
                   jax.ShapeDtypeStruct((n, C_HEADS * HEAD_PAD), BF16),
                   jax.ShapeDtypeStruct((C_HEADS * C_V, n), BF16)],
        compiler_params=_params(1),
        name="mla_in",
    )(x2, g, w1, qg, wab, kvg, wk, wv, ct, st)


def _mla_attn_kernel(qt_ref, k_ref, vt_ref, o_ref, vx_ref, sa_ref, sb_ref, *, tq):
    seq = k_ref.shape[1]
    hp = HEAD_PAD
    n_tiles = seq // tq
    n_blocks = seq // KEY_BLOCK
    n_items = ATTN_PAIRS * n_tiles
    ones_rows = V_EXT_ROWS - C_V
    for hd in range(2 * ATTN_PAIRS):
        vx_ref[hd, 0:C_V, :] = vt_ref[hd * C_V:(hd + 1) * C_V, :]
        vx_ref[hd, C_V:V_EXT_ROWS, :] = jnp.ones((ones_rows, seq), BF16)

    def keys(j):
        return slice(j * KEY_BLOCK, (j + 1) * KEY_BLOCK)

    def stage(score_item, score_ref, value_item, value_ref, m_prev):
        maxima = [None, None]
        accs = [None, None]
        if score_ref is not None:
            sp = score_item // n_tiles
            srows = pl.ds(pl.multiple_of((score_item % n_tiles) * tq, tq), tq)
            slanes = [pl.ds(pl.multiple_of((2 * sp + hh) * hp, hp), hp) for hh in range(2)]
            qts = [qt_ref[slanes[hh], srows] for hh in range(2)]
        if value_ref is not None:
            vp = value_item // n_tiles
            vrows = pl.ds(pl.multiple_of((value_item % n_tiles) * tq, tq), tq)
        for j in range(n_blocks):
            for hh in range(2):
                if score_ref is not None:
                    st = _dot(k_ref[0, keys(j), slanes[hh]], qts[hh])
                    score_ref[hh, keys(j), :] = st
                    m = jnp.max(st, axis=0, keepdims=True)
                    maxima[hh] = m if j == 0 else jnp.maximum(maxima[hh], m)
                if value_ref is not None:
                    pt = jnp.exp2(value_ref[hh, keys(j), :] - m_prev[hh]).astype(BF16)
                    o = _dot(vx_ref[2 * vp + hh, :, keys(j)], pt)
                    accs[hh] = o if j == 0 else accs[hh] + o
        if value_ref is not None:
            outs = [acc[0:C_V, :] / acc[C_V:C_V + 1, :] for acc in accs]
            olanes = pl.ds(pl.multiple_of(vp * 2 * C_V, 2 * C_V), 2 * C_V)
            o_ref[0, vrows, olanes] = jnp.concatenate(outs, axis=0).T.astype(BF16)
        return tuple(maxima)

    def pair_body(i, m_prev):
        m_mid = stage(2 * i + 1, sb_ref, 2 * i, sa_ref, m_prev)
        return stage(2 * i + 2, sa_ref, 2 * i + 1, sb_ref, m_mid)

    m_even = lax.fori_loop(0, n_items // 2 - 1, pair_body, stage(0, sa_ref, None, None, None))
    m_odd = stage(n_items - 1, sb_ref, n_items - 2, sa_ref, m_even)
    stage(None, None, n_items - 1, sb_ref, m_odd)


def _mla_attn(qt, k, vt, tq=256):
    bsz, seq, _ = k.shape
    heads = 2 * ATTN_PAIRS
    scores = pltpu.VMEM((2, seq, tq), F32)
    return pl.pallas_call(
        functools.partial(_mla_attn_kernel, tq=tq),
        grid=(bsz, C_HEADS // heads),
        in_specs=[pl.BlockSpec((heads * HEAD_PAD, seq), lambda b, h: (h, b)),
                  pl.BlockSpec((1, seq, heads * HEAD_PAD), lambda b, h: (b, 0, h)),
                  pl.BlockSpec((heads * C_V, seq), lambda b, h: (h, b))],
        out_specs=pl.BlockSpec((1, seq, heads * C_V), lambda b, h: (b, 0, h)),
        out_shape=jax.ShapeDtypeStruct((bsz, seq, C_HEADS * C_V), BF16),
        scratch_shapes=[pltpu.VMEM((heads, V_EXT_ROWS, seq), BF16),
                        scores, scores],
        compiler_params=_params(2),
        name="mla_attn",
    )(qt, k, vt)


def _mix_ffn_kernel(*refs, n_mix, tf, final):
    x_ref = refs[0]
    mix_refs = refs[1:1 + n_mix]
    wmix_ref, g_ref, w1_ref, w2_ref = refs[1 + n_mix:5 + n_mix]
    fg_ref = refs[5 + n_mix] if final else None
    o_ref = refs[-1]

    mix = jnp.concatenate([a_ref[...] for a_ref in mix_refs], axis=1)
    x = x_ref[...] + _dot(mix, wmix_ref[...])
    h = _rms(x, g_ref[...]).astype(BF16)
    for j in range(D_FF // tf):
        a = _dot(h, w1_ref[:, j * tf:(j + 1) * tf])
        a = jnp.square(jnp.maximum(a, 0.0)).astype(BF16)
        x = x + _dot(a, w2_ref[j * tf:(j + 1) * tf, :])
    if final:
        x = _rms(x, fg_ref[...])
    o_ref[...] = x


def _mix_ffn(x2, mixes, wmix, g, w1, w2, final_g=None, tm=1024, tf=512):
    n = x2.shape[0]
    row = lambda w: pl.BlockSpec((tm, w), lambda i: (i, 0))
    final = final_g is not None
    args = [x2, *mixes, wmix, g, w1, w2] + ([final_g] if final else [])
    in_specs = ([row(D_MODEL)] + [row(a.shape[1]) for a in mixes]
                + [_const_spec(a.shape) for a in args[1 + len(mixes):]])
    return pl.pallas_call(
        functools.partial(_mix_ffn_kernel, n_mix=len(mixes), tf=tf, final=final),
        grid=(n // tm,),
        in_specs=in_specs,
        out_specs=row(D_MODEL),
        out_shape=jax.ShapeDtypeStruct((n, D_MODEL), F32),
        compiler_params=_params(1),
        name="mix_ffn",
    )(*args)


def _pad_heads(w, width, parts):
    k = w.shape[0]
    w = w.reshape(k, C_HEADS, width)
    cols = [w[:, :, a:b] if a is not None else jnp.zeros((k, C_HEADS, b), w.dtype) for a, b in parts]
    used = sum(c.shape[2] for c in cols)
    cols.append(jnp.zeros((k, C_HEADS, HEAD_PAD - used), w.dtype))
    return jnp.concatenate(cols, axis=2).reshape(k, C_HEADS * HEAD_PAD)


def kernel(x, positions, ab_norm, ab_w_in, ab_conv, ab_gate_b, ab_head_g, ab_v_g, ab_ws, ab_bs,
           ab_w_out, c_norm, c_w_in, c_q_g, c_kv_g, c_w_uq, c_w_ukv, c_w_out, ffn_norm, ffn_w1,
           ffn_w2, final_norm):
    bsz, seq, _ = x.shape
    n = bsz * seq
    nc = seq // CHUNK
    half = C_ROPE // 2
    x2 = x.reshape(n, D_MODEL)

    freq = ROPE_BASE ** (-jnp.arange(half, dtype=F32) / half)
    gap = HEAD_PAD // 2 - half
    cos_t, sin_t = _rope_tables(positions.reshape(1, n), freq.reshape(half, 1))

    for layer in range(DEPTH):
        j = layer // 2
        final_g = final_norm.reshape(1, D_MODEL) if layer == DEPTH - 1 else None
        ffn_args = (ffn_norm[layer].reshape(1, D_MODEL), ffn_w1[layer].astype(BF16),
                    ffn_w2[layer].astype(BF16))
        if layer % 2 == 0:
            o0 = 4 * A_WIDTH
            o1 = o0 + 4 * A_HEADS
            wg = ab_w_in[j, :, o0:o1].reshape(D_MODEL, 4, A_HEADS).transpose(0, 2, 1)
            wg = jnp.pad(wg.reshape(D_MODEL, 4 * A_HEADS), ((0, 0), (0, LANES - 4 * A_HEADS))).astype(BF16)
            gb = jnp.pad(ab_gate_b[j].reshape(4, A_HEADS).T.reshape(1, 4 * A_HEADS),
                         ((0, 0), (0, LANES - 4 * A_HEADS)))
            qk, va, oa, gates_t, hb = _ab_in(
                x2, ab_norm[j].reshape(1, D_MODEL), ab_w_in[j, :, :o0].astype(BF16), wg, gb,
                ab_w_in[j, :, o1:].astype(BF16), ab_v_g[j].reshape(1, B_WIDTH),
                ab_ws[j].astype(BF16), ab_bs[j].T)
            ha = _mlstm(qk.reshape(bsz, seq, 2 * A_WIDTH), va.reshape(bsz, seq, A_WIDTH),
                        oa.reshape(bsz, seq, A_WIDTH), gates_t, ab_conv[j],
                        ab_head_g[j].reshape(1, A_WIDTH))
            x2 = _mix_ffn(x2, [ha.reshape(n, A_WIDTH), hb], ab_w_out[j].astype(BF16),
                          *ffn_args, final_g=final_g)
        else:
            w_in = c_w_in[j]
            kr1 = w_in[:, Q_LORA + KV_LORA:Q_LORA + KV_LORA + half]
            kr2 = w_in[:, Q_LORA + KV_LORA + half:]
            zc = lambda w: jnp.zeros((D_MODEL, w), w_in.dtype)
            w1 = jnp.concatenate([w_in[:, :Q_LORA + KV_LORA], kr1, zc(gap), kr2, zc(gap)],
                                 axis=1).astype(BF16)
            qw = C_NOPE + C_ROPE
            wab = _pad_heads(c_w_uq[j], qw, [(C_NOPE, C_NOPE + half), (0, gap), (C_NOPE + half, qw),
                                             (gap, C_NOPE)]).T.astype(BF16)
            wk = _pad_heads(c_w_ukv[j], C_NOPE + C_V, [(None, half), (0, gap), (None, half),
                                                       (gap, C_NOPE)]).astype(BF16)
            wv = c_w_ukv[j].reshape(KV_LORA, C_HEADS, C_NOPE + C_V)[:, :, C_NOPE:]
            wv_t = wv.reshape(KV_LORA, C_HEADS * C_V).T.astype(BF16)
            qt, k, vt = _mla_in(x2, c_norm[j].reshape(1, D_MODEL), w1, c_q_g[j].reshape(1, Q_LORA), wab,
                                c_kv_g[j].reshape(1, KV_LORA), wk, wv_t, cos_t, sin_t)
            o = _mla_attn(qt, k.reshape(bsz, seq, -1), vt)
            x2 = _mix_ffn(x2, [o.reshape(n, C_HEADS * C_V)], c_w_out[j].astype(BF16),
                          *ffn_args, final_g=final_g)
    return x2.reshape(bsz, seq, D_MODEL)
```

```python
import functools

import jax
import jax.numpy as jnp
from jax import lax
from jax.experimental import pallas as pl
from jax.experimental.pallas import tpu as pltpu

F32 = jnp.float32
BF16 = jnp.bfloat16

D_MODEL = 1024
DEPTH = 4
A_WIDTH = 512
A_HEADS = 4
A_HEAD_DIM = 128
CHUNK = 128
CONV_WIDTH = 5
B_WIDTH = 512
B_GROUPS = 4
B_GROUP_DIM = 128
C_HEADS = 16
C_NOPE = 64
C_ROPE = 32
C_V = 64
Q_LORA = 384
KV_LORA = 256
ROPE_BASE = 10000.0
D_FF = 4 * D_MODEL
EPS = 1e-6

LANES = 128
HEAD_PAD = 128
V_EXT_ROWS = C_V + 16
LOG2E = 1.4426950408889634
KEY_BLOCK = 256
ATTN_PAIRS = 2
VMEM_LIMIT = 56 * 1024 * 1024


def _params(n_axes):
    return pltpu.CompilerParams(
        dimension_semantics=("arbitrary",) * n_axes, vmem_limit_bytes=VMEM_LIMIT)


def _const_spec(shape):
    nd = len(shape)
    return pl.BlockSpec(shape, lambda *_: (0,) * nd, pipeline_mode=pl.Buffered(1))


def _rms(x, g):
    return x * lax.rsqrt(jnp.mean(x * x, axis=-1, keepdims=True) + EPS) * g


def _dot(a, b):
    return jnp.dot(a, b, preferred_element_type=F32)


def _sigmoid(x):
    return 1.0 / (1.0 + jnp.exp(-x))


def _ab_in_kernel(x_ref, g_ref, wa_ref, wg_ref, gb_ref, wuv_ref, vg_ref, ws_ref, bs_ref,
                  qk_ref, va_ref, oa_ref, gt_ref, hb_ref):
    tm = x_ref.shape[0]
    h = _rms(x_ref[...], g_ref[...]).astype(BF16)
    qk_ref[...] = _dot(h, wa_ref[:, 0:2 * A_WIDTH])
    va_ref[...] = _dot(h, wa_ref[:, 2 * A_WIDTH:3 * A_WIDTH]).astype(BF16)
    oa_ref[...] = _dot(h, wa_ref[:, 3 * A_WIDTH:4 * A_WIDTH])
    gt_ref[...] = (_dot(h, wg_ref[...]) + gb_ref[...]).T[0:4 * A_HEADS, :]

    uv = _dot(h, wuv_ref[...])
    uv = 0.5 * uv * (1.0 + jnp.tanh(0.7978845608028654 * (uv + 0.044715 * (uv * uv * uv))))
    n_chunks = tm // CHUNK
    for g in range(B_GROUPS):
        lo = g * B_GROUP_DIM
        u = uv[:, lo:lo + B_GROUP_DIM]
        vb = uv[:, B_WIDTH + lo:B_WIDTH + lo + B_GROUP_DIM]
        vb = _rms(vb, vg_ref[:, lo:lo + B_GROUP_DIM]).astype(BF16)
        rhs = jnp.concatenate([vb[c * CHUNK:(c + 1) * CHUNK, :] for c in range(n_chunks)], axis=1)
        sp = _dot(ws_ref[g], rhs) + bs_ref[:, g:g + 1]
        for c in range(n_chunks):
            hb_ref[c * CHUNK:(c + 1) * CHUNK, lo:lo + B_GROUP_DIM] = (
                u[c * CHUNK:(c + 1) * CHUNK, :] * sp[:, c * B_GROUP_DIM:(c + 1) * B_GROUP_DIM]
            ).astype(BF16)


def _ab_in(x2, g, wa, wg, gb, wuv, vg, ws, bs_t, tm=1024):
    n = x2.shape[0]
    row = lambda w: pl.BlockSpec((tm, w), lambda i: (i, 0))
    return pl.pallas_call(
        _ab_in_kernel,
        grid=(n // tm,),
        in_specs=[row(D_MODEL), _const_spec(g.shape), _const_spec(wa.shape), _const_spec(wg.shape),
                  _const_spec(gb.shape), _const_spec(wuv.shape), _const_spec(vg.shape),
                  _const_spec(ws.shape), _const_spec(bs_t.shape)],
        out_specs=[row(2 * A_WIDTH), row(A_WIDTH), row(A_WIDTH),
                   pl.BlockSpec((4 * A_HEADS, tm), lambda i: (0, i)), row(B_WIDTH)],
        out_shape=[jax.ShapeDtypeStruct((n, 2 * A_WIDTH), F32),
                   jax.ShapeDtypeStruct((n, A_WIDTH), BF16),
                   jax.ShapeDtypeStruct((n, A_WIDTH), F32),
                   jax.ShapeDtypeStruct((4 * A_HEADS, n), F32),
                   jax.ShapeDtypeStruct((n, B_WIDTH), BF16)],
        compiler_params=_params(1),
        name="ab_in",
    )(x2, g, wa, wg, gb, wuv, vg, ws, bs_t)


def _split3_dot(x, tri):
    x0 = x.astype(BF16)
    r1 = x - x0.astype(F32)
    x1 = r1.astype(BF16)
    x2 = (r1 - x1.astype(F32)).astype(BF16)
    return _dot(x0, tri) + _dot(x1, tri) + _dot(x2, tri)


def _mlstm_kernel(q_ref, k_ref, v_ref, oa_ref, gt_ref, cwq_ref, cwk_ref, hg_ref, o_ref,
                  xp_ref, qs_ref, kt_ref, va_ref, a_ref, cum_ref, tot_ref, amx_ref, bcol_ref,
                  hf_ref, hb_ref, cf_ref, cb_ref):
    seq = q_ref.shape[1]
    nc = seq // CHUNK
    dh = A_HEAD_DIM
    halo = 8
    pad = (CONV_WIDTH - 1) // 2

    zeros_halo = jnp.zeros((halo, dh), F32)
    xp_ref[0:halo, :] = zeros_halo
    xp_ref[halo + seq:halo + seq + halo, :] = zeros_halo

    def conv_chunk(c, w):
        r0 = pl.multiple_of(c * CHUNK, CHUNK) + (halo - pad)
        y = xp_ref[pl.ds(r0, CHUNK), :] * w[0:1, :]
        for j in range(1, CONV_WIDTH):
            y = y + xp_ref[pl.ds(r0 + j, CHUNK), :] * w[j:j + 1, :]
        return y * _sigmoid(y)

    xp_ref[halo:halo + seq, :] = q_ref[0]
    wq = cwq_ref[...]

    def q_body(c, carry):
        r0 = pl.multiple_of(c * CHUNK, CHUNK)
        qs_ref[pl.ds(r0, CHUNK), :] = (conv_chunk(c, wq) * (dh ** -0.5)).astype(BF16)
        return carry

    lax.fori_loop(0, nc, q_body, 0, unroll=2)

    xp_ref[halo:halo + seq, :] = k_ref[0]
    wk = cwk_ref[...]

    def k_body(c, carry):
        r0 = pl.multiple_of(c * CHUNK, CHUNK)
        kt_ref[:, pl.ds(r0, CHUNK)] = conv_chunk(c, wk).T.astype(BF16)
        return carry

    lax.fori_loop(0, nc, k_body, 0, unroll=4)

    va_ref[:, 0:dh] = v_ref[0]
    va_ref[:, dh:2 * dh] = jnp.ones((seq, dh), BF16)

    a_ref[...] = jnp.zeros_like(a_ref)
    cum_ref[...] = jnp.zeros_like(cum_ref)
    head = pl.program_id(1)
    for kind, (table, d) in enumerate([(a_ref, 0), (cum_ref, 0), (a_ref, 1), (cum_ref, 1)]):
        gate = gt_ref[pl.ds(head * 4 + kind, 1), :]
        for c in range(nc):
            table[8 * c + d:8 * c + d + 1, :] = gate[:, c * CHUNK:(c + 1) * CHUNK]
    li = a_ref[...]
    fpre = cum_ref[...]
    lf = jnp.minimum(fpre, 0.0) - jnp.log1p(jnp.exp(-jnp.abs(fpre)))
    direction = lax.broadcasted_iota(jnp.int32, li.shape, 0) % 8
    ui = lax.broadcasted_iota(jnp.int32, (CHUNK, CHUNK), 0)
    si = lax.broadcasted_iota(jnp.int32, (CHUNK, CHUNK), 1)
    cum = (_split3_dot(jnp.where(direction == 0, lf, 0.0), jnp.where(ui <= si, 1.0, 0.0).astype(BF16))
           + _split3_dot(jnp.where(direction == 1, lf, 0.0), jnp.where(ui >= si, 1.0, 0.0).astype(BF16)))
    a_tab = li - cum
    total = jnp.where(direction == 0, cum[:, CHUNK - 1:CHUNK], cum[:, 0:1])
    a_ref[...] = a_tab
    tot_ref[...] = jnp.broadcast_to(total, a_tab.shape)
    amx_ref[...] = jnp.broadcast_to(jnp.max(a_tab, axis=-1, keepdims=True), a_tab.shape)

    cum_ref[...] = cum

    def bcol_body(c, carry):
        rs = pl.ds(pl.multiple_of(c * CHUNK, CHUNK), CHUNK)
        tile = cum_ref[pl.ds(pl.multiple_of(c * 8, 8), 8), :]
        for d in range(2):
            bcol_ref[d, rs, :] = jnp.broadcast_to(tile[d:d + 1, :], (CHUNK, CHUNK)).T
        return carry

    lax.fori_loop(0, nc, bcol_body, 0, unroll=4)

    cf_ref[...] = jnp.zeros_like(cf_ref)
    cb_ref[...] = jnp.zeros_like(cb_ref)

    def step(c, c_ref, m, d):
        r0 = pl.multiple_of(c * CHUNK, CHUNK)
        g0 = pl.multiple_of(c * 8, 8)
        qc = qs_ref[pl.ds(r0, CHUNK), :]
        ktc = kt_ref[:, pl.ds(r0, CHUNK)]
        vc = va_ref[pl.ds(r0, CHUNK), :]
        a = a_ref[pl.ds(g0, 8), :][d:d + 1, :]
        tot = tot_ref[pl.ds(g0, 8), :][d:d + 1, :]
        amax = amx_ref[pl.ds(g0, 8), :][d:d + 1, :]
        visible = (ui >= si) if d == 0 else (ui <= si)
        am = jnp.where(visible, a, -jnp.inf)
        reach = jnp.broadcast_to(jnp.max(am, axis=-1, keepdims=True), (CHUNK, CHUNK))
        m_row = jnp.maximum(m, reach)
        inter = jnp.exp(m - m_row)
        s = _dot(qc, ktc) * jnp.exp(am - m_row)
        m_new = tot + jnp.maximum(m, amax)
        decay = jnp.exp(tot + m - m_new)
        kw = (ktc.astype(F32) * jnp.exp(tot + a - m_new)).astype(BF16)
        iq = (qc.astype(F32) * inter).astype(BF16)
        lhs = jnp.concatenate([jnp.concatenate([s.astype(BF16), iq], axis=1),
                               jnp.concatenate([kw, jnp.zeros_like(kw)], axis=1)], axis=0)
        nd_u = _dot(lhs, jnp.concatenate([vc, c_ref[...].astype(BF16)], axis=0))
        clamp = jnp.exp(-(bcol_ref[d, pl.ds(r0, CHUNK), :] + m_row))
        h = nd_u[0:CHUNK, 0:dh] / jnp.maximum(jnp.abs(nd_u[0:CHUNK, dh:2 * dh]), clamp)
        c_ref[...] = jnp.concatenate([decay, decay], axis=1) * c_ref[...] + nd_u[CHUNK:2 * CHUNK, :]
        return h, m_new

    def scan_body(c, carry):
        m_f, m_b = carry
        h_f, m_f = step(c, cf_ref, m_f, 0)
        hf_ref[pl.ds(pl.multiple_of(c * CHUNK, CHUNK), CHUNK), :] = h_f
        cb = nc - 1 - c
        h_b, m_b = step(cb, cb_ref, m_b, 1)
        hb_ref[pl.ds(pl.multiple_of(cb * CHUNK, CHUNK), CHUNK), :] = h_b
        return m_f, m_b

    m_init = jnp.zeros((1, CHUNK), F32)
    lax.fori_loop(0, nc, scan_body, (m_init, m_init), unroll=8)

    def out_body(c, carry):
        rs = pl.ds(pl.multiple_of(c * CHUNK, CHUNK), CHUNK)
        y = _rms(hf_ref[rs, :] + hb_ref[rs, :], hg_ref[...])
        o_ref[0, rs, :] = (y * _sigmoid(oa_ref[0, rs, :])).astype(BF16)
        return carry

    lax.fori_loop(0, nc, out_body, 0, unroll=4)


def _mlstm(qk, va, oa, gt, conv_w, head_g):
    bsz, seq, _ = qk.shape
    nc = seq // CHUNK
    dh = A_HEAD_DIM
    col = lambda off: pl.BlockSpec((1, seq, dh), lambda b, h: (b, 0, h + off))
    table = pltpu.VMEM((8 * nc, CHUNK), F32)
    return pl.pallas_call(
        _mlstm_kernel,
        grid=(bsz, A_HEADS),
        in_specs=[col(0), col(A_HEADS), col(0), col(0),
                  pl.BlockSpec((4 * A_HEADS, seq), lambda b, h: (0, b)),
                  pl.BlockSpec((CONV_WIDTH, dh), lambda b, h: (0, h)),
                  pl.BlockSpec((CONV_WIDTH, dh), lambda b, h: (0, h + A_HEADS)),
                  pl.BlockSpec((1, dh), lambda b, h: (0, h))],
        out_specs=col(0),
        out_shape=jax.ShapeDtypeStruct((bsz, seq, A_WIDTH), BF16),
        scratch_shapes=[pltpu.VMEM((seq + 16, dh), F32),
                        pltpu.VMEM((seq, dh), BF16),
                        pltpu.VMEM((dh, seq), BF16),
                        pltpu.VMEM((seq, 2 * dh), BF16),
                        table,
                        table,
                        table,
                        table,
                        pltpu.VMEM((2, seq, CHUNK), F32),
                        pltpu.VMEM((seq, dh), F32),
                        pltpu.VMEM((seq, dh), F32),
                        pltpu.VMEM((dh, 2 * dh), F32),
                        pltpu.VMEM((dh, 2 * dh), F32)],
        compiler_params=_params(2),
        name="mlstm",
    )(qk, qk, va, oa, gt, conv_w, conv_w, head_g)


def _rope_table_kernel(pos_ref, freq_ref, ct_ref, st_ref):
    half = C_ROPE // 2
    hp = HEAD_PAD
    ang = pos_ref[...].astype(F32) * freq_ref[...]
    cs = jnp.cos(ang)
    sn = jnp.sin(ang)
    ct_ref[...] = jnp.ones_like(ct_ref)
    st_ref[...] = jnp.zeros_like(st_ref)
    ct_ref[0:half, :] = cs
    ct_ref[hp // 2:hp // 2 + half, :] = cs
    st_ref[0:half, :] = -sn
    st_ref[hp // 2:hp // 2 + half, :] = sn


def _rope_tables(pos_row, freq_col, tn=2048):
    n = pos_row.shape[1]
    tn = min(tn, n)
    out_spec = pl.BlockSpec((HEAD_PAD, tn), lambda i: (0, i))
    out = jax.ShapeDtypeStruct((HEAD_PAD, n), F32)
    return pl.pallas_call(
        _rope_table_kernel,
        grid=(n // tn,),
        in_specs=[pl.BlockSpec((1, tn), lambda i: (0, i)), _const_spec(freq_col.shape)],
        out_specs=[out_spec, out_spec],
        out_shape=[out, out],
        compiler_params=_params(1),
        name="rope_tables",
    )(pos_row, freq_col)


def _mla_in_kernel(x_ref, g_ref, w1_ref, qg_ref, wq_ref, kvg_ref, wk_ref, wv_ref, ct_ref, st_ref,
                   q_ref, k_ref, v_ref):
    hp = HEAD_PAD
    h = _rms(x_ref[...], g_ref[...]).astype(BF16)
    p1 = _dot(h, w1_ref[...])
    ct = ct_ref[...]
    st = st_ref[...]

    def rotary(a):
        partner = jnp.concatenate([a[hp // 2:, :], a[:hp // 2, :]], axis=0)
        return a * ct + partner * st

    k_rope = rotary(p1[:, Q_LORA + KV_LORA:].T).T

    cq = _rms(p1[:, :Q_LORA], qg_ref[...]).astype(BF16)
    qa = lax.dot_general(wq_ref[...], cq, (((1,), (1,)), ((), ())), preferred_element_type=F32)
    scale = (C_NOPE + C_ROPE) ** -0.5 * LOG2E
    for hd in range(C_HEADS):
        q_ref[hd * hp:(hd + 1) * hp, :] = (rotary(qa[hd * hp:(hd + 1) * hp, :]) * scale).astype(BF16)

    ckv = _rms(p1[:, Q_LORA:Q_LORA + KV_LORA], kvg_ref[...]).astype(BF16)
    kn = _dot(ckv, wk_ref[...])
    for hd in range(C_HEADS):
        k_ref[:, hd * hp:(hd + 1) * hp] = (kn[:, hd * hp:(hd + 1) * hp] + k_rope).astype(BF16)
    v_ref[...] = lax.dot_general(wv_ref[...], ckv, (((1,), (1,)), ((), ())),
                                 preferred_element_type=F32).astype(BF16)


def _mla_in(x2, g, w1, qg, wab, kvg, wk, wv, ct, st, tm=1024):
    n = x2.shape[0]
    row = lambda w: pl.BlockSpec((tm, w), lambda i: (i, 0))
    col = lambda h: pl.BlockSpec((h, tm), lambda i: (0, i))
    return pl.pallas_call(
        _mla_in_kernel,
        grid=(n // tm,),
        in_specs=[row(D_MODEL), _const_spec(g.shape), _const_spec(w1.shape), _const_spec(qg.shape),
                  _const_spec(wab.shape), _const_spec(kvg.shape), _const_spec(wk.shape),
                  _const_spec(wv.shape), col(HEAD_PAD), col(HEAD_PAD)],
        out_specs=[col(C_HEADS * HEAD_PAD), row(C_HEADS * HEAD_PAD), col(C_HEADS * C_V)],
        out_shape=[jax.ShapeDtypeStruct((C_HEADS * HEAD_PAD, n), BF16),
                   jax.ShapeDtypeStruct((n, C_HEADS * HEAD_PAD), BF16),
                   jax.ShapeDtypeStruct((C_HEADS * C_V, n), BF16)],
        compiler_params=_params(1),
        name="mla_in",
    )(x2, g, w1, qg, wab, kvg, wk, wv, ct, st)


def _mla_attn_kernel(qt_ref, k_ref, vt_ref, o_ref, vx_ref, sa_ref, sb_ref, *, tq):
    seq = k_ref.shape[1]
    hp = HEAD_PAD
    n_tiles = seq // tq
    n_blocks = seq // KEY_BLOCK
    n_items = ATTN_PAIRS * n_tiles
    ones_rows = V_EXT_ROWS - C_V
    for hd in range(2 * ATTN_PAIRS):
        vx_ref[hd, 0:C_V, :] = vt_ref[hd * C_V:(hd + 1) * C_V, :]
        vx_ref[hd, C_V:V_EXT_ROWS, :] = jnp.ones((ones_rows, seq), BF16)

    def keys(j):
        return slice(j * KEY_BLOCK, (j + 1) * KEY_BLOCK)

    def stage(score_item, score_ref, value_item, value_ref, m_prev):
        maxima = [None, None]
        accs = [None, None]
        if score_ref is not None:
            sp = score_item // n_tiles
            srows = pl.ds(pl.multiple_of((score_item % n_tiles) * tq, tq), tq)
            slanes = [pl.ds(pl.multiple_of((2 * sp + hh) * hp, hp), hp) for hh in range(2)]
            qts = [qt_ref[slanes[hh], srows] for hh in range(2)]
        if value_ref is not None:
            vp = value_item // n_tiles
            vrows = pl.ds(pl.multiple_of((value_item % n_tiles) * tq, tq), tq)
        for j in range(n_blocks):
            for hh in range(2):
                if score_ref is not None:
                    st = _dot(k_ref[0, keys(j), slanes[hh]], qts[hh])
                    score_ref[hh, keys(j), :] = st
                    m = jnp.max(st, axis=0, keepdims=True)
                    maxima[hh] = m if j == 0 else jnp.maximum(maxima[hh], m)
                if value_ref is not None:
                    pt = jnp.exp2(value_ref[hh, keys(j), :] - m_prev[hh]).astype(BF16)
                    o = _dot(vx_ref[2 * vp + hh, :, keys(j)], pt)
                    accs[hh] = o if j == 0 else accs[hh] + o
        if value_ref is not None:
            outs = [acc[0:C_V, :] / acc[C_V:C_V + 1, :] for acc in accs]
            olanes = pl.ds(pl.multiple_of(vp * 2 * C_V, 2 * C_V), 2 * C_V)
            o_ref[0, vrows, olanes] = jnp.concatenate(outs, axis=0).T.astype(BF16)
        return tuple(maxima)

    def pair_body(i, m_prev):
        m_mid = stage(2 * i + 1, sb_ref, 2 * i, sa_ref, m_prev)
        return stage(2 * i + 2, sa_ref, 2 * i + 1, sb_ref, m_mid)

    m_even = lax.fori_loop(0, n_items // 2 - 1, pair_body, stage(0, sa_ref, None, None, None))
    m_odd = stage(n_items - 1, sb_ref, n_items - 2, sa_ref, m_even)
    stage(None, None, n_items - 1, sb_ref, m_odd)


def _mla_attn(qt, k, vt, tq=256):
    bsz, seq, _ = k.shape
    heads = 2 * ATTN_PAIRS
    scores = pltpu.VMEM((2, seq, tq), F32)
    return pl.pallas_call(
        functools.partial(_mla_attn_kernel, tq=tq),
        grid=(bsz, C_HEADS // heads),
        in_specs=[pl.BlockSpec((heads * HEAD_PAD, seq), lambda b, h: (h, b)),
                  pl.BlockSpec((1, seq, heads * HEAD_PAD), lambda b, h: (b, 0, h)),
                  pl.BlockSpec((heads * C_V, seq), lambda b, h: (h, b))],
        out_specs=pl.BlockSpec((1, seq, heads * C_V), lambda b, h: (b, 0, h)),
        out_shape=jax.ShapeDtypeStruct((bsz, seq, C_HEADS * C_V), BF16),
        scratch_shapes=[pltpu.VMEM((heads, V_EXT_ROWS, seq), BF16),
                        scores, scores],
        compiler_params=_params(2),
        name="mla_attn",
    )(qt, k, vt)


def _mix_ffn_kernel(*refs, n_mix, tf, final):
    x_ref = refs[0]
    mix_refs = refs[1:1 + n_mix]
    wmix_ref, g_ref, w1_ref, w2_ref = refs[1 + n_mix:5 + n_mix]
    fg_ref = refs[5 + n_mix] if final else None
    o_ref = refs[-1]

    mix = jnp.concatenate([a_ref[...] for a_ref in mix_refs], axis=1)
    x = x_ref[...] + _dot(mix, wmix_ref[...])
    h = _rms(x, g_ref[...]).astype(BF16)
    for j in range(D_FF // tf):
        a = _dot(h, w1_ref[:, j * tf:(j + 1) * tf])
        a = jnp.square(jnp.maximum(a, 0.0)).astype(BF16)
        x = x + _dot(a, w2_ref[j * tf:(j + 1) * tf, :])
    if final:
        x = _rms(x, fg_ref[...])
    o_ref[...] = x


def _mix_ffn(x2, mixes, wmix, g, w1, w2, final_g=None, tm=1024, tf=512):
    n = x2.shape[0]
    row = lambda w: pl.BlockSpec((tm, w), lambda i: (i, 0))
    final = final_g is not None
    args = [x2, *mixes, wmix, g, w1, w2] + ([final_g] if final else [])
    in_specs = ([row(D_MODEL)] + [row(a.shape[1]) for a in mixes]
                + [_const_spec(a.shape) for a in args[1 + len(mixes):]])
    return pl.pallas_call(
        functools.partial(_mix_ffn_kernel, n_mix=len(mixes), tf=tf, final=final),
        grid=(n // tm,),
        in_specs=in_specs,
        out_specs=row(D_MODEL),
        out_shape=jax.ShapeDtypeStruct((n, D_MODEL), F32),
        compiler_params=_params(1),
        name="mix_ffn",
    )(*args)


def _pad_heads(w, width, parts):
    k = w.shape[0]
    w = w.reshape(k, C_HEADS, width)
    cols = [w[:, :, a:b] if a is not None else jnp.zeros((k, C_HEADS, b), w.dtype) for a, b in parts]
    used = sum(c.shape[2] for c in cols)
    cols.append(jnp.zeros((k, C_HEADS, HEAD_PAD - used), w.dtype))
    return jnp.concatenate(cols, axis=2).reshape(k, C_HEADS * HEAD_PAD)


def kernel(x, positions, ab_norm, ab_w_in, ab_conv, ab_gate_b, ab_head_g, ab_v_g, ab_ws, ab_bs,
           ab_w_out, c_norm, c_w_in, c_q_g, c_kv_g, c_w_uq, c_w_ukv, c_w_out, ffn_norm, ffn_w1,
           ffn_w2, final_norm):
    bsz, seq, _ = x.shape
    n = bsz * seq
    half = C_ROPE // 2
    x2 = x.reshape(n, D_MODEL)

    freq = ROPE_BASE ** (-jnp.arange(half, dtype=F32) / half)
    gap = HEAD_PAD // 2 - half
    cos_t, sin_t = _rope_tables(positions.reshape(1, n), freq.reshape(half, 1))

    for layer in range(DEPTH):
        j = layer // 2
        final_g = final_norm.reshape(1, D_MODEL) if layer == DEPTH - 1 else None
        ffn_args = (ffn_norm[layer].reshape(1, D_MODEL), ffn_w1[layer].astype(BF16),
                    ffn_w2[layer].astype(BF16))
        if layer % 2 == 0:
            o0 = 4 * A_WIDTH
            o1 = o0 + 4 * A_HEADS
            wg = ab_w_in[j, :, o0:o1].reshape(D_MODEL, 4, A_HEADS).transpose(0, 2, 1)
            wg = jnp.pad(wg.reshape(D_MODEL, 4 * A_HEADS), ((0, 0), (0, LANES - 4 * A_HEADS))).astype(BF16)
            gb = jnp.pad(ab_gate_b[j].reshape(4, A_HEADS).T.reshape(1, 4 * A_HEADS),
                         ((0, 0), (0, LANES - 4 * A_HEADS)))
            qk, va, oa, gates_t, hb = _ab_in(
                x2, ab_norm[j].reshape(1, D_MODEL), ab_w_in[j, :, :o0].astype(BF16), wg, gb,
                ab_w_in[j, :, o1:].astype(BF16), ab_v_g[j].reshape(1, B_WIDTH),
                ab_ws[j].astype(BF16), ab_bs[j].T)
            ha = _mlstm(qk.reshape(bsz, seq, 2 * A_WIDTH), va.reshape(bsz, seq, A_WIDTH),
                        oa.reshape(bsz, seq, A_WIDTH), gates_t, ab_conv[j],
                        ab_head_g[j].reshape(1, A_WIDTH))
            x2 = _mix_ffn(x2, [ha.reshape(n, A_WIDTH), hb], ab_w_out[j].astype(BF16),
                          *ffn_args, final_g=final_g)
        else:
            w_in = c_w_in[j]
            kr1 = w_in[:, Q_LORA + KV_LORA:Q_LORA + KV_LORA + half]
            kr2 = w_in[:, Q_LORA + KV_LORA + half:]
            zc = lambda w: jnp.zeros((D_MODEL, w), w_in.dtype)
            w1 = jnp.concatenate([w_in[:, :Q_LORA + KV_LORA], kr1, zc(gap), kr2, zc(gap)],
                                 axis=1).astype(BF16)
            qw = C_NOPE + C_ROPE
            wab = _pad_heads(c_w_uq[j], qw, [(C_NOPE, C_NOPE + half), (0, gap), (C_NOPE + half, qw),
                                             (gap, C_NOPE)]).T.astype(BF16)
            wk = _pad_heads(c_w_ukv[j], C_NOPE + C_V, [(None, half), (0, gap), (None, half),
                                                       (gap, C_NOPE)]).astype(BF16)
            wv = c_w_ukv[j].reshape(KV_LORA, C_HEADS, C_NOPE + C_V)[:, :, C_NOPE:]
            wv_t = wv.reshape(KV_LORA, C_HEADS * C_V).T.astype(BF16)
            qt, k, vt = _mla_in(x2, c_norm[j].reshape(1, D_MODEL), w1, c_q_g[j].reshape(1, Q_LORA), wab,
                                c_kv_g[j].reshape(1, KV_LORA), wk, wv_t, cos_t, sin_t)
            o = _mla_attn(qt, k.reshape(bsz, seq, -1), vt)
            x2 = _mix_ffn(x2, [o.reshape(n, C_HEADS * C_V)], c_w_out[j].astype(BF16),
                          *ffn_args, final_g=final_g)
    return x2.reshape(bsz, seq, D_MODEL)
```

```python
import functools

import jax
import jax.numpy as jnp
from jax import lax
from jax.experimental import pallas as pl
from jax.experimental.pallas import tpu as pltpu

F32 = jnp.float32
BF16 = jnp.bfloat16

D_MODEL = 1024
DEPTH = 4
A_WIDTH = 512
A_HEADS = 4
A_HEAD_DIM = 128
CHUNK = 128
CONV_WIDTH = 5
B_WIDTH = 512
B_GROUPS = 4
B_GROUP_DIM = 128
C_HEADS = 16
C_NOPE = 64
C_ROPE = 32
C_V = 64
Q_LORA = 384
KV_LORA = 256
ROPE_BASE = 10000.0
D_FF = 4 * D_MODEL
EPS = 1e-6

LANES = 128
HEAD_PAD = 128
V_EXT_ROWS = C_V + 16
LOG2E = 1.4426950408889634
KEY_BLOCK = 256
ATTN_PAIRS = 2
VMEM_LIMIT = 56 * 1024 * 1024


def _params(n_axes):
    return pltpu.CompilerParams(
        dimension_semantics=("arbitrary",) * n_axes, vmem_limit_bytes=VMEM_LIMIT)


def _const_spec(shape):
    nd = len(shape)
    return pl.BlockSpec(shape, lambda *_: (0,) * nd, pipeline_mode=pl.Buffered(1))


def _rms(x, g):
    return x * lax.rsqrt(jnp.mean(x * x, axis=-1, keepdims=True) + EPS) * g


def _dot(a, b):
    return jnp.dot(a, b, preferred_element_type=F32)


def _sigmoid(x):
    return 1.0 / (1.0 + jnp.exp(-x))


def _ab_in_kernel(x_ref, g_ref, wa_ref, wg_ref, gb_ref, wuv_ref, vg_ref, ws_ref, bs_ref,
                  qk_ref, va_ref, oa_ref, gt_ref, hb_ref):
    tm = x_ref.shape[0]
    h = _rms(x_ref[...], g_ref[...]).astype(BF16)
    qk_ref[...] = _dot(h, wa_ref[:, 0:2 * A_WIDTH])
    va_ref[...] = _dot(h, wa_ref[:, 2 * A_WIDTH:3 * A_WIDTH]).astype(BF16)
    oa_ref[...] = _dot(h, wa_ref[:, 3 * A_WIDTH:4 * A_WIDTH])
    gt_ref[...] = (_dot(h, wg_ref[...]) + gb_ref[...]).T[0:4 * A_HEADS, :]

    uv = _dot(h, wuv_ref[...])
    uv = 0.5 * uv * (1.0 + jnp.tanh(0.7978845608028654 * (uv + 0.044715 * (uv * uv * uv))))
    n_chunks = tm // CHUNK
    for g in range(B_GROUPS):
        lo = g * B_GROUP_DIM
        u = uv[:, lo:lo + B_GROUP_DIM]
        vb = uv[:, B_WIDTH + lo:B_WIDTH + lo + B_GROUP_DIM]
        vb = _rms(vb, vg_ref[:, lo:lo + B_GROUP_DIM]).astype(BF16)
        rhs = jnp.concatenate([vb[c * CHUNK:(c + 1) * CHUNK, :] for c in range(n_chunks)], axis=1)
        sp = _dot(ws_ref[g], rhs) + bs_ref[:, g:g + 1]
        for c in range(n_chunks):
            hb_ref[c * CHUNK:(c + 1) * CHUNK, lo:lo + B_GROUP_DIM] = (
                u[c * CHUNK:(c + 1) * CHUNK, :] * sp[:, c * B_GROUP_DIM:(c + 1) * B_GROUP_DIM]
            ).astype(BF16)


def _ab_in(x2, g, wa, wg, gb, wuv, vg, ws, bs_t, tm=1024):
    n = x2.shape[0]
    row = lambda w: pl.BlockSpec((tm, w), lambda i: (i, 0))
    return pl.pallas_call(
        _ab_in_kernel,
        grid=(n // tm,),
        in_specs=[row(D_MODEL), _const_spec(g.shape), _const_spec(wa.shape), _const_spec(wg.shape),
                  _const_spec(gb.shape), _const_spec(wuv.shape), _const_spec(vg.shape),
                  _const_spec(ws.shape), _const_spec(bs_t.shape)],
        out_specs=[row(2 * A_WIDTH), row(A_WIDTH), row(A_WIDTH),
                   pl.BlockSpec((4 * A_HEADS, tm), lambda i: (0, i)), row(B_WIDTH)],
        out_shape=[jax.ShapeDtypeStruct((n, 2 * A_WIDTH), F32),
                   jax.ShapeDtypeStruct((n, A_WIDTH), BF16),
                   jax.ShapeDtypeStruct((n, A_WIDTH), F32),
                   jax.ShapeDtypeStruct((4 * A_HEADS, n), F32),
                   jax.ShapeDtypeStruct((n, B_WIDTH), BF16)],
        compiler_params=_params(1),
        name="ab_in",
    )(x2, g, wa, wg, gb, wuv, vg, ws, bs_t)


def _split3_dot(x, tri):
    x0 = x.astype(BF16)
    r1 = x - x0.astype(F32)
    x1 = r1.astype(BF16)
    x2 = (r1 - x1.astype(F32)).astype(BF16)
    return _dot(x0, tri) + _dot(x1, tri) + _dot(x2, tri)


def _mlstm_kernel(q_ref, k_ref, v_ref, oa_ref, gt_ref, cwq_ref, cwk_ref, hg_ref, o_ref,
                  xp_ref, qs_ref, kt_ref, va_ref, a_ref, cum_ref, tot_ref, amx_ref,
                  hf_ref, hb_ref, cf_ref, cb_ref):
    seq = q_ref.shape[1]
    nc = seq // CHUNK
    dh = A_HEAD_DIM
    halo = 8
    pad = (CONV_WIDTH - 1) // 2

    zeros_halo = jnp.zeros((halo, dh), F32)
    xp_ref[0:halo, :] = zeros_halo
    xp_ref[halo + seq:halo + seq + halo, :] = zeros_halo

    def conv_chunk(c, w):
        r0 = pl.multiple_of(c * CHUNK, CHUNK) + (halo - pad)
        y = xp_ref[pl.ds(r0, CHUNK), :] * w[0:1, :]
        for j in range(1, CONV_WIDTH):
            y = y + xp_ref[pl.ds(r0 + j, CHUNK), :] * w[j:j + 1, :]
        return y * _sigmoid(y)

    xp_ref[halo:halo + seq, :] = q_ref[0]
    wq = cwq_ref[...]

    def q_body(c, carry):
        r0 = pl.multiple_of(c * CHUNK, CHUNK)
        qs_ref[pl.ds(r0, CHUNK), :] = (conv_chunk(c, wq) * (dh ** -0.5)).astype(BF16)
        return carry

    lax.fori_loop(0, nc, q_body, 0, unroll=2)

    xp_ref[halo:halo + seq, :] = k_ref[0]
    wk = cwk_ref[...]

    def k_body(c, carry):
        r0 = pl.multiple_of(c * CHUNK, CHUNK)
        kt_ref[:, pl.ds(r0, CHUNK)] = conv_chunk(c, wk).T.astype(BF16)
        return carry

    lax.fori_loop(0, nc, k_body, 0, unroll=4)

    va_ref[:, 0:dh] = v_ref[0]
    va_ref[:, dh:2 * dh] = jnp.ones((seq, dh), BF16)

    a_ref[...] = jnp.zeros_like(a_ref)
    cum_ref[...] = jnp.zeros_like(cum_ref)
    head = pl.program_id(1)
    for kind, (table, d) in enumerate([(a_ref, 0), (cum_ref, 0), (a_ref, 1), (cum_ref, 1)]):
        gate = gt_ref[pl.ds(head * 4 + kind, 1), :]
        for c in range(nc):
            table[8 * c + d:8 * c + d + 1, :] = gate[:, c * CHUNK:(c + 1) * CHUNK]
    li = a_ref[...]
    fpre = cum_ref[...]
    lf = jnp.minimum(fpre, 0.0) - jnp.log1p(jnp.exp(-jnp.abs(fpre)))
    direction = lax.broadcasted_iota(jnp.int32, li.shape, 0) % 8
    ui = lax.broadcasted_iota(jnp.int32, (CHUNK, CHUNK), 0)
    si = lax.broadcasted_iota(jnp.int32, (CHUNK, CHUNK), 1)
    cum = (_split3_dot(jnp.where(direction == 0, lf, 0.0), jnp.where(ui <= si, 1.0, 0.0).astype(BF16))
           + _split3_dot(jnp.where(direction == 1, lf, 0.0), jnp.where(ui >= si, 1.0, 0.0).astype(BF16)))
    a_tab = li - cum
    total = jnp.where(direction == 0, cum[:, CHUNK - 1:CHUNK], cum[:, 0:1])
    a_ref[...] = a_tab
    tot_ref[...] = jnp.broadcast_to(total, a_tab.shape)
    amx_ref[...] = jnp.broadcast_to(jnp.max(a_tab, axis=-1, keepdims=True), a_tab.shape)

    cum_ref[...] = cum

    cf_ref[...] = jnp.zeros_like(cf_ref)
    cb_ref[...] = jnp.zeros_like(cb_ref)

    def step(c, c_ref, m, d):
        r0 = pl.multiple_of(c * CHUNK, CHUNK)
        g0 = pl.multiple_of(c * 8, 8)
        qc = qs_ref[pl.ds(r0, CHUNK), :]
        ktc = kt_ref[:, pl.ds(r0, CHUNK)]
        vc = va_ref[pl.ds(r0, CHUNK), :]
        a = a_ref[pl.ds(g0, 8), :][d:d + 1, :]
        tot = tot_ref[pl.ds(g0, 8), :][d:d + 1, :]
        amax = amx_ref[pl.ds(g0, 8), :][d:d + 1, :]
        visible = (ui >= si) if d == 0 else (ui <= si)
        am = jnp.where(visible, a, -jnp.inf)
        reach = jnp.broadcast_to(jnp.max(am, axis=-1, keepdims=True), (CHUNK, CHUNK))
        m_row = jnp.maximum(m, reach)
        inter = jnp.exp(m - m_row)
        s = _dot(qc, ktc) * jnp.exp(am - m_row)
        m_new = tot + jnp.maximum(m, amax)
        decay = jnp.exp(tot + m - m_new)
        kw = (ktc.astype(F32) * jnp.exp(tot + a - m_new)).astype(BF16)
        iq = (qc.astype(F32) * inter).astype(BF16)
        lhs = jnp.concatenate([jnp.concatenate([kw, jnp.zeros_like(kw)], axis=1),
                               jnp.concatenate([s.astype(BF16), iq], axis=1)], axis=0)
        u_nd = _dot(lhs, jnp.concatenate([vc, c_ref[...].astype(BF16)], axis=0))
        c_ref[...] = jnp.concatenate([decay, decay], axis=1) * c_ref[...] + u_nd[0:CHUNK, :]
        b_col = jnp.broadcast_to(cum_ref[pl.ds(g0, 8), :][d:d + 1, :], (CHUNK, CHUNK)).T
        clamp = jnp.exp(-(b_col + m_row))
        h = u_nd[CHUNK:, 0:dh] / jnp.maximum(jnp.abs(u_nd[CHUNK:, dh:2 * dh]), clamp)
        return h, m_new

    def scan_body(c, carry):
        m_f, m_b = carry
        h_f, m_f = step(c, cf_ref, m_f, 0)
        hf_ref[pl.ds(pl.multiple_of(c * CHUNK, CHUNK), CHUNK), :] = h_f
        cb = nc - 1 - c
        h_b, m_b = step(cb, cb_ref, m_b, 1)
        hb_ref[pl.ds(pl.multiple_of(cb * CHUNK, CHUNK), CHUNK), :] = h_b
        return m_f, m_b

    m_init = jnp.zeros((1, CHUNK), F32)
    lax.fori_loop(0, nc, scan_body, (m_init, m_init), unroll=8)

    def out_body(c, carry):
        rs = pl.ds(pl.multiple_of(c * CHUNK, CHUNK), CHUNK)
        y = _rms(hf_ref[rs, :] + hb_ref[rs, :], hg_ref[...])
        o_ref[0, rs, :] = (y * _sigmoid(oa_ref[0, rs, :])).astype(BF16)
        return carry

    lax.fori_loop(0, nc, out_body, 0, unroll=4)


def _mlstm(qk, va, oa, gt, conv_w, head_g):
    bsz, seq, _ = qk.shape
    nc = seq // CHUNK
    dh = A_HEAD_DIM
    col = lambda off: pl.BlockSpec((1, seq, dh), lambda b, h: (b, 0, h + off))
    table = pltpu.VMEM((8 * nc, CHUNK), F32)
    return pl.pallas_call(
        _mlstm_kernel,
        grid=(bsz, A_HEADS),
        in_specs=[col(0), col(A_HEADS), col(0), col(0),
                  pl.BlockSpec((4 * A_HEADS, seq), lambda b, h: (0, b)),
                  pl.BlockSpec((CONV_WIDTH, dh), lambda b, h: (0, h)),
                  pl.BlockSpec((CONV_WIDTH, dh), lambda b, h: (0, h + A_HEADS)),
                  pl.BlockSpec((1, dh), lambda b, h: (0, h))],
        out_specs=col(0),
        out_shape=jax.ShapeDtypeStruct((bsz, seq, A_WIDTH), BF16),
        scratch_shapes=[pltpu.VMEM((seq + 16, dh), F32),
                        pltpu.VMEM((seq, dh), BF16),
                        pltpu.VMEM((dh, seq), BF16),
                        pltpu.VMEM((seq, 2 * dh), BF16),
                        table,
                        table,
                        table,
                        table,
                        pltpu.VMEM((seq, dh), F32),
                        pltpu.VMEM((seq, dh), F32),
                        pltpu.VMEM((dh, 2 * dh), F32),
                        pltpu.VMEM((dh, 2 * dh), F32)],
        compiler_params=_params(2),
        name="mlstm",
    )(qk, qk, va, oa, gt, conv_w, conv_w, head_g)


def _rope_table_kernel(pos_ref, freq_ref, ct_ref, st_ref):
    half = C_ROPE // 2
    hp = HEAD_PAD
    ang = pos_ref[...].astype(F32) * freq_ref[...]
    cs = jnp.cos(ang)
    sn = jnp.sin(ang)
    ct_ref[...] = jnp.ones_like(ct_ref)
    st_ref[...] = jnp.zeros_like(st_ref)
    ct_ref[0:half, :] = cs
    ct_ref[hp // 2:hp // 2 + half, :] = cs
    st_ref[0:half, :] = -sn
    st_ref[hp // 2:hp // 2 + half, :] = sn


def _rope_tables(pos_row, freq_col, tn=2048):
    n = pos_row.shape[1]
    tn = min(tn, n)
    out_spec = pl.BlockSpec((HEAD_PAD, tn), lambda i: (0, i))
    out = jax.ShapeDtypeStruct((HEAD_PAD, n), F32)
    return pl.pallas_call(
        _rope_table_kernel,
        grid=(n // tn,),
        in_specs=[pl.BlockSpec((1, tn), lambda i: (0, i)), _const_spec(freq_col.shape)],
        out_specs=[out_spec, out_spec],
        out_shape=[out, out],
        compiler_params=_params(1),
        name="rope_tables",
    )(pos_row, freq_col)


def _mla_in_kernel(x_ref, g_ref, w1_ref, qg_ref, wq_ref, kvg_ref, wk_ref, wv_ref, ct_ref, st_ref,
                   q_ref, k_ref, v_ref):
    hp = HEAD_PAD
    h = _rms(x_ref[...], g_ref[...]).astype(BF16)
    p1 = _dot(h, w1_ref[...])
    ct = ct_ref[...]
    st = st_ref[...]

    def rotary(a):
        partner = jnp.concatenate([a[hp // 2:, :], a[:hp // 2, :]], axis=0)
        return a * ct + partner * st

    k_rope = rotary(p1[:, Q_LORA + KV_LORA:].T).T

    cq = _rms(p1[:, :Q_LORA], qg_ref[...]).astype(BF16)
    qa = lax.dot_general(wq_ref[...], cq, (((1,), (1,)), ((), ())), preferred_element_type=F32)
    scale = (C_NOPE + C_ROPE) ** -0.5 * LOG2E
    for hd in range(C_HEADS):
        q_ref[hd * hp:(hd + 1) * hp, :] = (rotary(qa[hd * hp:(hd + 1) * hp, :]) * scale).astype(BF16)

    ckv = _rms(p1[:, Q_LORA:Q_LORA + KV_LORA], kvg_ref[...]).astype(BF16)
    kn = _dot(ckv, wk_ref[...])
    for hd in range(C_HEADS):
        k_ref[:, hd * hp:(hd + 1) * hp] = (kn[:, hd * hp:(hd + 1) * hp] + k_rope).astype(BF16)
    v_ref[...] = lax.dot_general(wv_ref[...], ckv, (((1,), (1,)), ((), ())),
                                 preferred_element_type=F32).astype(BF16)


def _mla_in(x2, g, w1, qg, wab, kvg, wk, wv, ct, st, tm=1024):
    n = x2.shape[0]
    row = lambda w: pl.BlockSpec((tm, w), lambda i: (i, 0))
    col = lambda h: pl.BlockSpec((h, tm), lambda i: (0, i))
    return pl.pallas_call(
        _mla_in_kernel,
        grid=(n // tm,),
        in_specs=[row(D_MODEL), _const_spec(g.shape), _const_spec(w1.shape), _const_spec(qg.shape),
                  _const_spec(wab.shape), _const_spec(kvg.shape), _const_spec(wk.shape),
                  _const_spec(wv.shape), col(HEAD_PAD), col(HEAD_PAD)],
        out_specs=[col(C_HEADS * HEAD_PAD), row(C_HEADS * HEAD_PAD), col(C_HEADS * C_V)],
        out_shape=[jax.ShapeDtypeStruct((C_HEADS * HEAD_PAD, n), BF16),
                   jax.ShapeDtypeStruct((n, C_HEADS * HEAD_PAD), BF16),
                   jax.ShapeDtypeStruct((C_HEADS * C_V, n), BF16)],
        compiler_params=_params(1),
        name="mla_in",
    )(x2, g, w1, qg, wab, kvg, wk, wv, ct, st)


def _mla_attn_kernel(qt_ref, k_ref, vt_ref, o_ref, vx_ref, sa_ref, sb_ref, *, tq):
    seq = k_ref.shape[1]
    hp = HEAD_PAD
    n_tiles = seq // tq
    n_blocks = seq // KEY_BLOCK
    n_items = ATTN_PAIRS * n_tiles
    ones_rows = V_EXT_ROWS - C_V
    for hd in range(2 * ATTN_PAIRS):
        vx_ref[hd, 0:C_V, :] = vt_ref[hd * C_V:(hd + 1) * C_V, :]
        vx_ref[hd, C_V:V_EXT_ROWS, :] = jnp.ones((ones_rows, seq), BF16)

    def keys(j):
        return slice(j * KEY_BLOCK, (j + 1) * KEY_BLOCK)

    def stage(score_item, score_ref, value_item, value_ref, m_prev):
        maxima = [None, None]
        accs = [None, None]
        if score_ref is not None:
            sp = score_item // n_tiles
            srows = pl.ds(pl.multiple_of((score_item % n_tiles) * tq, tq), tq)
            slanes = [pl.ds(pl.multiple_of((2 * sp + hh) * hp, hp), hp) for hh in range(2)]
            qts = [qt_ref[slanes[hh], srows] for hh in range(2)]
        if value_ref is not None:
            vp = value_item // n_tiles
            vrows = pl.ds(pl.multiple_of((value_item % n_tiles) * tq, tq), tq)
        for j in range(n_blocks):
            for hh in range(2):
                if score_ref is not None:
                    st = _dot(k_ref[0, keys(j), slanes[hh]], qts[hh])
                    score_ref[hh, keys(j), :] = st
                    m = jnp.max(st, axis=0, keepdims=True)
                    maxima[hh] = m if j == 0 else jnp.maximum(maxima[hh], m)
                if value_ref is not None:
                    pt = jnp.exp2(value_ref[hh, keys(j), :] - m_prev[hh]).astype(BF16)
                    o = _dot(vx_ref[2 * vp + hh, :, keys(j)], pt)
                    accs[hh] = o if j == 0 else accs[hh] + o
        if value_ref is not None:
            outs = [acc[0:C_V, :] / acc[C_V:C_V + 1, :] for acc in accs]
            olanes = pl.ds(pl.multiple_of(vp * 2 * C_V, 2 * C_V), 2 * C_V)
            o_ref[0, vrows, olanes] = jnp.concatenate(outs, axis=0).T.astype(BF16)
        return tuple(maxima)

    def pair_body(i, m_prev):
        m_mid = stage(2 * i + 1, sb_ref, 2 * i, sa_ref, m_prev)
        return stage(2 * i + 2, sa_ref, 2 * i + 1, sb_ref, m_mid)

    m_even = lax.fori_loop(0, n_items // 2 - 1, pair_body, stage(0, sa_ref, None, None, None))
    m_odd = stage(n_items - 1, sb_ref, n_items - 2, sa_ref, m_even)
    stage(None, None, n_items - 1, sb_ref, m_odd)


def _mla_attn(qt, k, vt, tq=256):
    bsz, seq, _ = k.shape
    heads = 2 * ATTN_PAIRS
    scores = pltpu.VMEM((2, seq, tq), F32)
    return pl.pallas_call(
        functools.partial(_mla_attn_kernel, tq=tq),
        grid=(bsz, C_HEADS // heads),
        in_specs=[pl.BlockSpec((heads * HEAD_PAD, seq), lambda b, h: (h, b)),
                  pl.BlockSpec((1, seq, heads * HEAD_PAD), lambda b, h: (b, 0, h)),
                  pl.BlockSpec((heads * C_V, seq), lambda b, h: (h, b))],
        out_specs=pl.BlockSpec((1, seq, heads * C_V), lambda b, h: (b, 0, h)),
        out_shape=jax.ShapeDtypeStruct((bsz, seq, C_HEADS * C_V), BF16),
        scratch_shapes=[pltpu.VMEM((heads, V_EXT_ROWS, seq), BF16),
                        scores, scores],
        compiler_params=_params(2),
        name="mla_attn",
    )(qt, k, vt)


def _mix_ffn_kernel(*refs, n_mix, tf, final):
    x_ref = refs[0]
    mix_refs = refs[1:1 + n_mix]
    wmix_ref, g_ref, w1_ref, w2_ref = refs[1 + n_mix:5 + n_mix]
    fg_ref = refs[5 + n_mix] if final else None
    o_ref = refs[-1]

    mix = jnp.concatenate([a_ref[...] for a_ref in mix_refs], axis=1)
    x = x_ref[...] + _dot(mix, wmix_ref[...])
    h = _rms(x, g_ref[...]).astype(BF16)
    for j in range(D_FF // tf):
        a = _dot(h, w1_ref[:, j * tf:(j + 1) * tf])
        a = jnp.square(jnp.maximum(a, 0.0)).astype(BF16)
        x = x + _dot(a, w2_ref[j * tf:(j + 1) * tf, :])
    if final:
        x = _rms(x, fg_ref[...])
    o_ref[...] = x


def _mix_ffn(x2, mixes, wmix, g, w1, w2, final_g=None, tm=1024, tf=512):
    n = x2.shape[0]
    row = lambda w: pl.BlockSpec((tm, w), lambda i: (i, 0))
    final = final_g is not None
    args = [x2, *mixes, wmix, g, w1, w2] + ([final_g] if final else [])
    in_specs = ([row(D_MODEL)] + [row(a.shape[1]) for a in mixes]
                + [_const_spec(a.shape) for a in args[1 + len(mixes):]])
    return pl.pallas_call(
        functools.partial(_mix_ffn_kernel, n_mix=len(mixes), tf=tf, final=final),
        grid=(n // tm,),
        in_specs=in_specs,
        out_specs=row(D_MODEL),
        out_shape=jax.ShapeDtypeStruct((n, D_MODEL), F32),
        compiler_params=_params(1),
        name="mix_ffn",
    )(*args)


def _pad_heads(w, width, parts):
    k = w.shape[0]
    w = w.reshape(k, C_HEADS, width)
    cols = [w[:, :, a:b] if a is not None else jnp.zeros((k, C_HEADS, b), w.dtype) for a, b in parts]
    used = sum(c.shape[2] for c in cols)
    cols.append(jnp.zeros((k, C_HEADS, HEAD_PAD - used), w.dtype))
    return jnp.concatenate(cols, axis=2).reshape(k, C_HEADS * HEAD_PAD)


def kernel(x, positions, ab_norm, ab_w_in, ab_conv, ab_gate_b, ab_head_g, ab_v_g, ab_ws, ab_bs,
           ab_w_out, c_norm, c_w_in, c_q_g, c_kv_g, c_w_uq, c_w_ukv, c_w_out, ffn_norm, ffn_w1,
           ffn_w2, final_norm):
    bsz, seq, _ = x.shape
    n = bsz * seq
    half = C_ROPE // 2
    x2 = x.reshape(n, D_MODEL)

    freq = ROPE_BASE ** (-jnp.arange(half, dtype=F32) / half)
    gap = HEAD_PAD // 2 - half
    cos_t, sin_t = _rope_tables(positions.reshape(1, n), freq.reshape(half, 1))

    for layer in range(DEPTH):
        j = layer // 2
        final_g = final_norm.reshape(1, D_MODEL) if layer == DEPTH - 1 else None
        ffn_args = (ffn_norm[layer].reshape(1, D_MODEL), ffn_w1[layer].astype(BF16),
                    ffn_w2[layer].astype(BF16))
        if layer % 2 == 0:
            o0 = 4 * A_WIDTH
            o1 = o0 + 4 * A_HEADS
            wg = ab_w_in[j, :, o0:o1].reshape(D_MODEL, 4, A_HEADS).transpose(0, 2, 1)
            wg = jnp.pad(wg.reshape(D_MODEL, 4 * A_HEADS), ((0, 0), (0, LANES - 4 * A_HEADS))).astype(BF16)
            gb = jnp.pad(ab_gate_b[j].reshape(4, A_HEADS).T.reshape(1, 4 * A_HEADS),
                         ((0, 0), (0, LANES - 4 * A_HEADS)))
            qk, va, oa, gates_t, hb = _ab_in(
                x2, ab_norm[j].reshape(1, D_MODEL), ab_w_in[j, :, :o0].astype(BF16), wg, gb,
                ab_w_in[j, :, o1:].astype(BF16), ab_v_g[j].reshape(1, B_WIDTH),
                ab_ws[j].astype(BF16), ab_bs[j].T)
            ha = _mlstm(qk.reshape(bsz, seq, 2 * A_WIDTH), va.reshape(bsz, seq, A_WIDTH),
                        oa.reshape(bsz, seq, A_WIDTH), gates_t, ab_conv[j],
                        ab_head_g[j].reshape(1, A_WIDTH))
            x2 = _mix_ffn(x2, [ha.reshape(n, A_WIDTH), hb], ab_w_out[j].astype(BF16),
                          *ffn_args, final_g=final_g)
        else:
            w_in = c_w_in[j]
            kr1 = w_in[:, Q_LORA + KV_LORA:Q_LORA + KV_LORA + half]
            kr2 = w_in[:, Q_LORA + KV_LORA + half:]
            zc = lambda w: jnp.zeros((D_MODEL, w), w_in.dtype)
            w1 = jnp.concatenate([w_in[:, :Q_LORA + KV_LORA], kr1, zc(gap), kr2, zc(gap)],
                                 axis=1).astype(BF16)
            qw = C_NOPE + C_ROPE
            wab = _pad_heads(c_w_uq[j], qw, [(C_NOPE, C_NOPE + half), (0, gap), (C_NOPE + half, qw),
                                             (gap, C_NOPE)]).T.astype(BF16)
            wk = _pad_heads(c_w_ukv[j], C_NOPE + C_V, [(None, half), (0, gap), (None, half),
                                                       (gap, C_NOPE)]).astype(BF16)
            wv = c_w_ukv[j].reshape(KV_LORA, C_HEADS, C_NOPE + C_V)[:, :, C_NOPE:]
            wv_t = wv.reshape(KV_LORA, C_HEADS * C_V).T.astype(BF16)
            qt, k, vt = _mla_in(x2, c_norm[j].reshape(1, D_MODEL), w1, c_q_g[j].reshape(1, Q_LORA), wab,
                                c_kv_g[j].reshape(1, KV_LORA), wk, wv_t, cos_t, sin_t)
            o = _mla_attn(qt, k.reshape(bsz, seq, -1), vt)
            x2 = _mix_ffn(x2, [o.reshape(n, C_HEADS * C_V)], c_w_out[j].astype(BF16),
                          *ffn_args, final_g=final_g)
    return x2.reshape(bsz, seq, D_MODEL)
```

```python
import functools

import jax
import jax.numpy as jnp
from jax import lax
from jax.experimental import pallas as pl
from jax.experimental.pallas import tpu as pltpu

F32 = jnp.float32
BF16 = jnp.bfloat16

D_MODEL = 1024
DEPTH = 4
A_WIDTH = 512
A_HEADS = 4
A_HEAD_DIM = 128
CHUNK = 128
CONV_WIDTH = 5
B_WIDTH = 512
B_GROUPS = 4
B_GROUP_DIM = 128
C_HEADS = 16
C_NOPE = 64
C_ROPE = 32
C_V = 64
Q_LORA = 384
KV_LORA = 256
ROPE_BASE = 10000.0
D_FF = 4 * D_MODEL
EPS = 1e-6

LANES = 128
HEAD_PAD = 128
V_EXT_ROWS = C_V + 16
LOG2E = 1.4426950408889634
KEY_BLOCK = 256
ATTN_PAIRS = 2
VMEM_LIMIT = 56 * 1024 * 1024


def _params(n_axes):
    return pltpu.CompilerParams(
        dimension_semantics=("arbitrary",) * n_axes, vmem_limit_bytes=VMEM_LIMIT)


def _const_spec(shape):
    nd = len(shape)
    return pl.BlockSpec(shape, lambda *_: (0,) * nd, pipeline_mode=pl.Buffered(1))


def _rms(x, g):
    return x * lax.rsqrt(jnp.mean(x * x, axis=-1, keepdims=True) + EPS) * g


def _dot(a, b):
    return jnp.dot(a, b, preferred_element_type=F32)


def _sigmoid(x):
    return 1.0 / (1.0 + jnp.exp(-x))


def _ab_in_kernel(x_ref, g_ref, wa_ref, wg_ref, gb_ref, wuv_ref, vg_ref, ws_ref, bs_ref,
                  qk_ref, va_ref, oa_ref, gt_ref, hb_ref):
    tm = x_ref.shape[0]
    h = _rms(x_ref[...], g_ref[...]).astype(BF16)
    qk_ref[...] = _dot(h, wa_ref[:, 0:2 * A_WIDTH])
    va_ref[...] = _dot(h, wa_ref[:, 2 * A_WIDTH:3 * A_WIDTH]).astype(BF16)
    oa_ref[...] = _dot(h, wa_ref[:, 3 * A_WIDTH:4 * A_WIDTH])
    gt_ref[...] = (_dot(h, wg_ref[...]) + gb_ref[...]).T[0:4 * A_HEADS, :]

    uv = _dot(h, wuv_ref[...])
    uv = 0.5 * uv * (1.0 + jnp.tanh(0.7978845608028654 * (uv + 0.044715 * (uv * uv * uv))))
    n_chunks = tm // CHUNK
    for g in range(B_GROUPS):
        lo = g * B_GROUP_DIM
        u = uv[:, lo:lo + B_GROUP_DIM]
        vb = uv[:, B_WIDTH + lo:B_WIDTH + lo + B_GROUP_DIM]
        vb = _rms(vb, vg_ref[:, lo:lo + B_GROUP_DIM]).astype(BF16)
        rhs = jnp.concatenate([vb[c * CHUNK:(c + 1) * CHUNK, :] for c in range(n_chunks)], axis=1)
        sp = _dot(ws_ref[g], rhs) + bs_ref[:, g:g + 1]
        for c in range(n_chunks):
            hb_ref[c * CHUNK:(c + 1) * CHUNK, lo:lo + B_GROUP_DIM] = (
                u[c * CHUNK:(c + 1) * CHUNK, :] * sp[:, c * B_GROUP_DIM:(c + 1) * B_GROUP_DIM]
            ).astype(BF16)


def _ab_in(x2, g, wa, wg, gb, wuv, vg, ws, bs_t, tm=1024):
    n = x2.shape[0]
    row = lambda w: pl.BlockSpec((tm, w), lambda i: (i, 0))
    return pl.pallas_call(
        _ab_in_kernel,
        grid=(n // tm,),
        in_specs=[row(D_MODEL), _const_spec(g.shape), _const_spec(wa.shape), _const_spec(wg.shape),
                  _const_spec(gb.shape), _const_spec(wuv.shape), _const_spec(vg.shape),
                  _const_spec(ws.shape), _const_spec(bs_t.shape)],
        out_specs=[row(2 * A_WIDTH), row(A_WIDTH), row(A_WIDTH),
                   pl.BlockSpec((4 * A_HEADS, tm), lambda i: (0, i)), row(B_WIDTH)],
        out_shape=[jax.ShapeDtypeStruct((n, 2 * A_WIDTH), F32),
                   jax.ShapeDtypeStruct((n, A_WIDTH), BF16),
                   jax.ShapeDtypeStruct((n, A_WIDTH), F32),
                   jax.ShapeDtypeStruct((4 * A_HEADS, n), F32),
                   jax.ShapeDtypeStruct((n, B_WIDTH), BF16)],
        compiler_params=_params(1),
        name="ab_in",
    )(x2, g, wa, wg, gb, wuv, vg, ws, bs_t)


def _split3_dot(x, tri):
    x0 = x.astype(BF16)
    r1 = x - x0.astype(F32)
    x1 = r1.astype(BF16)
    x2 = (r1 - x1.astype(F32)).astype(BF16)
    return _dot(x0, tri) + _dot(x1, tri) + _dot(x2, tri)


def _mlstm_kernel(q_ref, k_ref, v_ref, oa_ref, gt_ref, cwq_ref, cwk_ref, hg_ref, o_ref,
                  xp_ref, qs_ref, kt_ref, va_ref, a_ref, cum_ref, tot_ref, amx_ref,
                  hf_ref, hb_ref, cf_ref, cb_ref):
    seq = q_ref.shape[1]
    nc = seq // CHUNK
    dh = A_HEAD_DIM
    halo = 8
    pad = (CONV_WIDTH - 1) // 2

    zeros_halo = jnp.zeros((halo, dh), F32)
    xp_ref[0:halo, :] = zeros_halo
    xp_ref[halo + seq:halo + seq + halo, :] = zeros_halo

    def conv_chunk(c, w):
        r0 = pl.multiple_of(c * CHUNK, CHUNK) + (halo - pad)
        y = xp_ref[pl.ds(r0, CHUNK), :] * w[0:1, :]
        for j in range(1, CONV_WIDTH):
            y = y + xp_ref[pl.ds(r0 + j, CHUNK), :] * w[j:j + 1, :]
        return y * _sigmoid(y)

    xp_ref[halo:halo + seq, :] = q_ref[0]
    wq = cwq_ref[...]

    def q_body(c, carry):
        r0 = pl.multiple_of(c * CHUNK, CHUNK)
        qs_ref[pl.ds(r0, CHUNK), :] = (conv_chunk(c, wq) * (dh ** -0.5)).astype(BF16)
        return carry

    lax.fori_loop(0, nc, q_body, 0, unroll=2)

    xp_ref[halo:halo + seq, :] = k_ref[0]
    wk = cwk_ref[...]

    def k_body(c, carry):
        r0 = pl.multiple_of(c * CHUNK, CHUNK)
        kt_ref[:, pl.ds(r0, CHUNK)] = conv_chunk(c, wk).T.astype(BF16)
        return carry

    lax.fori_loop(0, nc, k_body, 0, unroll=4)

    va_ref[:, 0:dh] = v_ref[0]
    va_ref[:, dh:2 * dh] = jnp.ones((seq, dh), BF16)

    a_ref[...] = jnp.zeros_like(a_ref)
    cum_ref[...] = jnp.zeros_like(cum_ref)
    head = pl.program_id(1)
    for kind, (table, d) in enumerate([(a_ref, 0), (cum_ref, 0), (a_ref, 1), (cum_ref, 1)]):
        gate = gt_ref[pl.ds(head * 4 + kind, 1), :]
        for c in range(nc):
            table[8 * c + d:8 * c + d + 1, :] = gate[:, c * CHUNK:(c + 1) * CHUNK]
    li = a_ref[...]
    fpre = cum_ref[...]
    lf = jnp.minimum(fpre, 0.0) - jnp.log1p(jnp.exp(-jnp.abs(fpre)))
    direction = lax.broadcasted_iota(jnp.int32, li.shape, 0) % 8
    ui = lax.broadcasted_iota(jnp.int32, (CHUNK, CHUNK), 0)
    si = lax.broadcasted_iota(jnp.int32, (CHUNK, CHUNK), 1)
    cum = (_split3_dot(jnp.where(direction == 0, lf, 0.0), jnp.where(ui <= si, 1.0, 0.0).astype(BF16))
           + _split3_dot(jnp.where(direction == 1, lf, 0.0), jnp.where(ui >= si, 1.0, 0.0).astype(BF16)))
    a_tab = li - cum
    total = jnp.where(direction == 0, cum[:, CHUNK - 1:CHUNK], cum[:, 0:1])
    a_ref[...] = a_tab
    tot_ref[...] = jnp.broadcast_to(total, a_tab.shape)
    amx_ref[...] = jnp.broadcast_to(jnp.max(a_tab, axis=-1, keepdims=True), a_tab.shape)

    cum_ref[...] = cum

    cf_ref[...] = jnp.zeros_like(cf_ref)
    cb_ref[...] = jnp.zeros_like(cb_ref)

    def step(c, c_ref, m, d):
        r0 = pl.multiple_of(c * CHUNK, CHUNK)
        g0 = pl.multiple_of(c * 8, 8)
        qc = qs_ref[pl.ds(r0, CHUNK), :]
        ktc = kt_ref[:, pl.ds(r0, CHUNK)]
        vc = va_ref[pl.ds(r0, CHUNK), :]
        a = a_ref[pl.ds(g0, 8), :][d:d + 1, :]
        tot = tot_ref[pl.ds(g0, 8), :][d:d + 1, :]
        amax = amx_ref[pl.ds(g0, 8), :][d:d + 1, :]
        visible = (ui >= si) if d == 0 else (ui <= si)
        am = jnp.where(visible, a, -jnp.inf)
        reach = jnp.broadcast_to(jnp.max(am, axis=-1, keepdims=True), (CHUNK, CHUNK))
        m_row = jnp.maximum(m, reach)
        inter = jnp.exp(m - m_row)
        s = _dot(qc, ktc) * jnp.exp(am - m_row)
        m_new = tot + jnp.maximum(m, amax)
        decay = jnp.exp(tot + m - m_new)
        kw = (ktc.astype(F32) * jnp.exp(tot + a - m_new)).astype(BF16)
        iq = (qc.astype(F32) * inter).astype(BF16)
        lhs = jnp.concatenate([jnp.concatenate([kw, jnp.zeros_like(kw)], axis=1),
                               jnp.concatenate([s.astype(BF16), iq], axis=1)], axis=0)
        u_nd = _dot(lhs, jnp.concatenate([vc, c_ref[...].astype(BF16)], axis=0))
        c_ref[...] = jnp.concatenate([decay, decay], axis=1) * c_ref[...] + u_nd[0:CHUNK, :]
        b_col = jnp.broadcast_to(cum_ref[pl.ds(g0, 8), :][d:d + 1, :], (CHUNK, CHUNK)).T
        clamp = jnp.exp(-(b_col + m_row))
        h = u_nd[CHUNK:, 0:dh] / jnp.maximum(jnp.abs(u_nd[CHUNK:, dh:2 * dh]), clamp)
        return h, m_new

    def emit(rs, h_sum):
        y = _rms(h_sum, hg_ref[...])
        o_ref[0, rs, :] = (y * _sigmoid(oa_ref[0, rs, :])).astype(BF16)

    def scan_body(c, carry, *, crossed):
        m_f, m_b = carry
        cb = nc - 1 - c
        rf = pl.ds(pl.multiple_of(c * CHUNK, CHUNK), CHUNK)
        rb = pl.ds(pl.multiple_of(cb * CHUNK, CHUNK), CHUNK)
        h_f, m_f = step(c, cf_ref, m_f, 0)
        h_b, m_b = step(cb, cb_ref, m_b, 1)
        if crossed:
            emit(rf, h_f + hb_ref[rf, :])
            emit(rb, hf_ref[rb, :] + h_b)
        else:
            hf_ref[rf, :] = h_f
            hb_ref[rb, :] = h_b
        return m_f, m_b

    assert nc % 2 == 0
    m_init = jnp.zeros((1, CHUNK), F32)
    m_half = lax.fori_loop(0, nc // 2, functools.partial(scan_body, crossed=False),
                           (m_init, m_init), unroll=8)
    lax.fori_loop(nc // 2, nc, functools.partial(scan_body, crossed=True), m_half, unroll=8)


def _mlstm(qk, va, oa, gt, conv_w, head_g):
    bsz, seq, _ = qk.shape
    nc = seq // CHUNK
    dh = A_HEAD_DIM
    col = lambda off: pl.BlockSpec((1, seq, dh), lambda b, h: (b, 0, h + off))
    table = pltpu.VMEM((8 * nc, CHUNK), F32)
    return pl.pallas_call(
        _mlstm_kernel,
        grid=(bsz, A_HEADS),
        in_specs=[col(0), col(A_HEADS), col(0), col(0),
                  pl.BlockSpec((4 * A_HEADS, seq), lambda b, h: (0, b)),
                  pl.BlockSpec((CONV_WIDTH, dh), lambda b, h: (0, h)),
                  pl.BlockSpec((CONV_WIDTH, dh), lambda b, h: (0, h + A_HEADS)),
                  pl.BlockSpec((1, dh), lambda b, h: (0, h))],
        out_specs=col(0),
        out_shape=jax.ShapeDtypeStruct((bsz, seq, A_WIDTH), BF16),
        scratch_shapes=[pltpu.VMEM((seq + 16, dh), F32),
                        pltpu.VMEM((seq, dh), BF16),
                        pltpu.VMEM((dh, seq), BF16),
                        pltpu.VMEM((seq, 2 * dh), BF16),
                        table,
                        table,
                        table,
                        table,
                        pltpu.VMEM((seq, dh), F32),
                        pltpu.VMEM((seq, dh), F32),
                        pltpu.VMEM((dh, 2 * dh), F32),
                        pltpu.VMEM((dh, 2 * dh), F32)],
        compiler_params=_params(2),
        name="mlstm",
    )(qk, qk, va, oa, gt, conv_w, conv_w, head_g)


def _rope_table_kernel(pos_ref, freq_ref, ct_ref, st_ref):
    half = C_ROPE // 2
    hp = HEAD_PAD
    ang = pos_ref[...].astype(F32) * freq_ref[...]
    cs = jnp.cos(ang)
    sn = jnp.sin(ang)
    ct_ref[...] = jnp.ones_like(ct_ref)
    st_ref[...] = jnp.zeros_like(st_ref)
    ct_ref[0:half, :] = cs
    ct_ref[hp // 2:hp // 2 + half, :] = cs
    st_ref[0:half, :] = -sn
    st_ref[hp // 2:hp // 2 + half, :] = sn


def _rope_tables(pos_row, freq_col, tn=2048):
    n = pos_row.shape[1]
    tn = min(tn, n)
    out_spec = pl.BlockSpec((HEAD_PAD, tn), lambda i: (0, i))
    out = jax.ShapeDtypeStruct((HEAD_PAD, n), F32)
    return pl.pallas_call(
        _rope_table_kernel,
        grid=(n // tn,),
        in_specs=[pl.BlockSpec((1, tn), lambda i: (0, i)), _const_spec(freq_col.shape)],
        out_specs=[out_spec, out_spec],
        out_shape=[out, out],
        compiler_params=_params(1),
        name="rope_tables",
    )(pos_row, freq_col)


def _mla_in_kernel(x_ref, g_ref, w1_ref, qg_ref, wq_ref, kvg_ref, wk_ref, wv_ref, ct_ref, st_ref,
                   q_ref, k_ref, v_ref):
    hp = HEAD_PAD
    h = _rms(x_ref[...], g_ref[...]).astype(BF16)
    p1 = _dot(h, w1_ref[...])
    ct = ct_ref[...]
    st = st_ref[...]

    def rotary(a):
        partner = jnp.concatenate([a[hp // 2:, :], a[:hp // 2, :]], axis=0)
        return a * ct + partner * st

    k_rope = rotary(p1[:, Q_LORA + KV_LORA:].T).T

    cq = _rms(p1[:, :Q_LORA], qg_ref[...]).astype(BF16)
    qa = lax.dot_general(wq_ref[...], cq, (((1,), (1,)), ((), ())), preferred_element_type=F32)
    scale = (C_NOPE + C_ROPE) ** -0.5 * LOG2E
    for hd in range(C_HEADS):
        q_ref[hd * hp:(hd + 1) * hp, :] = (rotary(qa[hd * hp:(hd + 1) * hp, :]) * scale).astype(BF16)

    ckv = _rms(p1[:, Q_LORA:Q_LORA + KV_LORA], kvg_ref[...]).astype(BF16)
    kn = _dot(ckv, wk_ref[...])
    for hd in range(C_HEADS):
        k_ref[:, hd * hp:(hd + 1) * hp] = (kn[:, hd * hp:(hd + 1) * hp] + k_rope).astype(BF16)
    v_ref[...] = lax.dot_general(wv_ref[...], ckv, (((1,), (1,)), ((), ())),
                                 preferred_element_type=F32).astype(BF16)


def _mla_in(x2, g, w1, qg, wab, kvg, wk, wv, ct, st, tm=1024):
    n = x2.shape[0]
    row = lambda w: pl.BlockSpec((tm, w), lambda i: (i, 0))
    col = lambda h: pl.BlockSpec((h, tm), lambda i: (0, i))
    return pl.pallas_call(
        _mla_in_kernel,
        grid=(n // tm,),
        in_specs=[row(D_MODEL), _const_spec(g.shape), _const_spec(w1.shape), _const_spec(qg.shape),
                  _const_spec(wab.shape), _const_spec(kvg.shape), _const_spec(wk.shape),
                  _const_spec(wv.shape), col(HEAD_PAD), col(HEAD_PAD)],
        out_specs=[col(C_HEADS * HEAD_PAD), row(C_HEADS * HEAD_PAD), col(C_HEADS * C_V)],
        out_shape=[jax.ShapeDtypeStruct((C_HEADS * HEAD_PAD, n), BF16),
                   jax.ShapeDtypeStruct((n, C_HEADS * HEAD_PAD), BF16),
                   jax.ShapeDtypeStruct((C_HEADS * C_V, n), BF16)],
        compiler_params=_params(1),
        name="mla_in",
    )(x2, g, w1, qg, wab, kvg, wk, wv, ct, st)


def _mla_attn_kernel(qt_ref, k_ref, vt_ref, o_ref, vx_ref, sa_ref, sb_ref, *, tq):
    seq = k_ref.shape[1]
    hp = HEAD_PAD
    n_tiles = seq // tq
    n_blocks = seq // KEY_BLOCK
    n_items = ATTN_PAIRS * n_tiles
    ones_rows = V_EXT_ROWS - C_V
    for hd in range(2 * ATTN_PAIRS):
        vx_ref[hd, 0:C_V, :] = vt_ref[hd * C_V:(hd + 1) * C_V, :]
        vx_ref[hd, C_V:V_EXT_ROWS, :] = jnp.ones((ones_rows, seq), BF16)

    def keys(j):
        return slice(j * KEY_BLOCK, (j + 1) * KEY_BLOCK)

    def stage(score_item, score_ref, value_item, value_ref, m_prev):
        maxima = [None, None]
        accs = [None, None]
        if score_ref is not None:
            sp = score_item // n_tiles
            srows = pl.ds(pl.multiple_of((score_item % n_tiles) * tq, tq), tq)
            slanes = [pl.ds(pl.multiple_of((2 * sp + hh) * hp, hp), hp) for hh in range(2)]
            qts = [qt_ref[slanes[hh], srows] for hh in range(2)]
        if value_ref is not None:
            vp = value_item // n_tiles
            vrows = pl.ds(pl.multiple_of((value_item % n_tiles) * tq, tq), tq)
        for j in range(n_blocks):
            for hh in range(2):
                if score_ref is not None:
                    st = _dot(k_ref[0, keys(j), slanes[hh]], qts[hh])
                    score_ref[hh, keys(j), :] = st
                    m = jnp.max(st, axis=0, keepdims=True)
                    maxima[hh] = m if j == 0 else jnp.maximum(maxima[hh], m)
                if value_ref is not None:
                    pt = jnp.exp2(value_ref[hh, keys(j), :] - m_prev[hh]).astype(BF16)
                    o = _dot(vx_ref[2 * vp + hh, :, keys(j)], pt)
                    accs[hh] = o if j == 0 else accs[hh] + o
        if value_ref is not None:
            outs = [acc[0:C_V, :] / acc[C_V:C_V + 1, :] for acc in accs]
            olanes = pl.ds(pl.multiple_of(vp * 2 * C_V, 2 * C_V), 2 * C_V)
            o_ref[0, vrows, olanes] = jnp.concatenate(outs, axis=0).T.astype(BF16)
        return tuple(maxima)

    def pair_body(i, m_prev):
        m_mid = stage(2 * i + 1, sb_ref, 2 * i, sa_ref, m_prev)
        return stage(2 * i + 2, sa_ref, 2 * i + 1, sb_ref, m_mid)

    m_even = lax.fori_loop(0, n_items // 2 - 1, pair_body, stage(0, sa_ref, None, None, None))
    m_odd = stage(n_items - 1, sb_ref, n_items - 2, sa_ref, m_even)
    stage(None, None, n_items - 1, sb_ref, m_odd)


def _mla_attn(qt, k, vt, tq=256):
    bsz, seq, _ = k.shape
    heads = 2 * ATTN_PAIRS
    scores = pltpu.VMEM((2, seq, tq), F32)
    return pl.pallas_call(
        functools.partial(_mla_attn_kernel, tq=tq),
        grid=(bsz, C_HEADS // heads),
        in_specs=[pl.BlockSpec((heads * HEAD_PAD, seq), lambda b, h: (h, b)),
                  pl.BlockSpec((1, seq, heads * HEAD_PAD), lambda b, h: (b, 0, h)),
                  pl.BlockSpec((heads * C_V, seq), lambda b, h: (h, b))],
        out_specs=pl.BlockSpec((1, seq, heads * C_V), lambda b, h: (b, 0, h)),
        out_shape=jax.ShapeDtypeStruct((bsz, seq, C_HEADS * C_V), BF16),
        scratch_shapes=[pltpu.VMEM((heads, V_EXT_ROWS, seq), BF16),
                        scores, scores],
        compiler_params=_params(2),
        name="mla_attn",
    )(qt, k, vt)


def _mix_ffn_kernel(*refs, n_mix, tf, final):
    x_ref = refs[0]
    mix_refs = refs[1:1 + n_mix]
    wmix_ref, g_ref, w1_ref, w2_ref = refs[1 + n_mix:5 + n_mix]
    fg_ref = refs[5 + n_mix] if final else None
    o_ref = refs[-1]

    mix = jnp.concatenate([a_ref[...] for a_ref in mix_refs], axis=1)
    x = x_ref[...] + _dot(mix, wmix_ref[...])
    h = _rms(x, g_ref[...]).astype(BF16)
    for j in range(D_FF // tf):
        a = _dot(h, w1_ref[:, j * tf:(j + 1) * tf])
        a = jnp.square(jnp.maximum(a, 0.0)).astype(BF16)
        x = x + _dot(a, w2_ref[j * tf:(j + 1) * tf, :])
    if final:
        x = _rms(x, fg_ref[...])
    o_ref[...] = x


def _mix_ffn(x2, mixes, wmix, g, w1, w2, final_g=None, tm=1024, tf=512):
    n = x2.shape[0]
    row = lambda w: pl.BlockSpec((tm, w), lambda i: (i, 0))
    final = final_g is not None
    args = [x2, *mixes, wmix, g, w1, w2] + ([final_g] if final else [])
    in_specs = ([row(D_MODEL)] + [row(a.shape[1]) for a in mixes]
                + [_const_spec(a.shape) for a in args[1 + len(mixes):]])
    return pl.pallas_call(
        functools.partial(_mix_ffn_kernel, n_mix=len(mixes), tf=tf, final=final),
        grid=(n // tm,),
        in_specs=in_specs,
        out_specs=row(D_MODEL),
        out_shape=jax.ShapeDtypeStruct((n, D_MODEL), F32),
        compiler_params=_params(1),
        name="mix_ffn",
    )(*args)


def _pad_heads(w, width, parts):
    k = w.shape[0]
    w = w.reshape(k, C_HEADS, width)
    cols = [w[:, :, a:b] if a is not None else jnp.zeros((k, C_HEADS, b), w.dtype) for a, b in parts]
    used = sum(c.shape[2] for c in cols)
    cols.append(jnp.zeros((k, C_HEADS, HEAD_PAD - used), w.dtype))
    return jnp.concatenate(cols, axis=2).reshape(k, C_HEADS * HEAD_PAD)


def kernel(x, positions, ab_norm, ab_w_in, ab_conv, ab_gate_b, ab_head_g, ab_v_g, ab_ws, ab_bs,
           ab_w_out, c_norm, c_w_in, c_q_g, c_kv_g, c_w_uq, c_w_ukv, c_w_out, ffn_norm, ffn_w1,
           ffn_w2, final_norm):
    bsz, seq, _ = x.shape
    n = bsz * seq
    half = C_ROPE // 2
    x2 = x.reshape(n, D_MODEL)

    freq = ROPE_BASE ** (-jnp.arange(half, dtype=F32) / half)
    gap = HEAD_PAD // 2 - half
    cos_t, sin_t = _rope_tables(positions.reshape(1, n), freq.reshape(half, 1))

    for layer in range(DEPTH):
        j = layer // 2
        final_g = final_norm.reshape(1, D_MODEL) if layer == DEPTH - 1 else None
        ffn_args = (ffn_norm[layer].reshape(1, D_MODEL), ffn_w1[layer].astype(BF16),
                    ffn_w2[layer].astype(BF16))
        if layer % 2 == 0:
            o0 = 4 * A_WIDTH
            o1 = o0 + 4 * A_HEADS
            wg = ab_w_in[j, :, o0:o1].reshape(D_MODEL, 4, A_HEADS).transpose(0, 2, 1)
            wg = jnp.pad(wg.reshape(D_MODEL, 4 * A_HEADS), ((0, 0), (0, LANES - 4 * A_HEADS))).astype(BF16)
            gb = jnp.pad(ab_gate_b[j].reshape(4, A_HEADS).T.reshape(1, 4 * A_HEADS),
                         ((0, 0), (0, LANES - 4 * A_HEADS)))
            qk, va, oa, gates_t, hb = _ab_in(
                x2, ab_norm[j].reshape(1, D_MODEL), ab_w_in[j, :, :o0].astype(BF16), wg, gb,
                ab_w_in[j, :, o1:].astype(BF16), ab_v_g[j].reshape(1, B_WIDTH),
                ab_ws[j].astype(BF16), ab_bs[j].T)
            ha = _mlstm(qk.reshape(bsz, seq, 2 * A_WIDTH), va.reshape(bsz, seq, A_WIDTH),
                        oa.reshape(bsz, seq, A_WIDTH), gates_t, ab_conv[j],
                        ab_head_g[j].reshape(1, A_WIDTH))
            x2 = _mix_ffn(x2, [ha.reshape(n, A_WIDTH), hb], ab_w_out[j].astype(BF16),
                          *ffn_args, final_g=final_g)
        else:
            w_in = c_w_in[j]
            kr1 = w_in[:, Q_LORA + KV_LORA:Q_LORA + KV_LORA + half]
            kr2 = w_in[:, Q_LORA + KV_LORA + half:]
            zc = lambda w: jnp.zeros((D_MODEL, w), w_in.dtype)
            w1 = jnp.concatenate([w_in[:, :Q_LORA + KV_LORA], kr1, zc(gap), kr2, zc(gap)],
                                 axis=1).astype(BF16)
            qw = C_NOPE + C_ROPE
            wab = _pad_heads(c_w_uq[j], qw, [(C_NOPE, C_NOPE + half), (0, gap), (C_NOPE + half, qw),
                                             (gap, C_NOPE)]).T.astype(BF16)
            wk = _pad_heads(c_w_ukv[j], C_NOPE + C_V, [(None, half), (0, gap), (None, half),
                                                       (gap, C_NOPE)]).astype(BF16)
            wv = c_w_ukv[j].reshape(KV_LORA, C_HEADS, C_NOPE + C_V)[:, :, C_NOPE:]
            wv_t = wv.reshape(KV_LORA, C_HEADS * C_V).T.astype(BF16)
            qt, k, vt = _mla_in(x2, c_norm[j].reshape(1, D_MODEL), w1, c_q_g[j].reshape(1, Q_LORA), wab,
                                c_kv_g[j].reshape(1, KV_LORA), wk, wv_t, cos_t, sin_t)
            o = _mla_attn(qt, k.reshape(bsz, seq, -1), vt)
            x2 = _mix_ffn(x2, [o.reshape(n, C_HEADS * C_V)], c_w_out[j].astype(BF16),
                          *ffn_args, final_g=final_g)
    return x2.reshape(bsz, seq, D_MODEL)
```
